```python
import jax, jax.numpy as jnp
from jax import lax
import numpy as np

D_MODEL = 1024
BATCH = 4
SEQ = 4096
DEPTH = 4

RET_HEADS = 4
RET_DK = 128
RET_DV = 128
FOX_HEADS = 8
FOX_DH = 64
D_FF = 2816
CHUNK = 128
Q_BLOCK = 128
ROPE_BASE = 10000.0
EPS = 1e-6

RET_QK = RET_HEADS * RET_DK
RET_V = RET_HEADS * RET_DV
FOX_W = FOX_HEADS * FOX_DH
IN_SPLITS = (RET_QK, RET_QK, RET_V, RET_V, FOX_W, FOX_W, FOX_W, FOX_HEADS, D_MODEL, D_MODEL)
IN_COLS = sum(IN_SPLITS)

kernel_name = "hybrid_retention_forgetting_attention_macaron"


def rmsnorm(x, g):
    xf = x.astype(jnp.float32)
    y = xf * lax.rsqrt(jnp.mean(xf * xf, axis=-1, keepdims=True) + EPS)
    return (y * g.astype(jnp.float32)).astype(x.dtype)


def swiglu(h, w_in, w_out):
    gu = h @ w_in
    a, b = gu[..., :D_FF], gu[..., D_FF:]
    return (jax.nn.silu(a) * b) @ w_out


def split_cols(p):
    outs = []
    off = 0
    for w in IN_SPLITS:
        outs.append(p[..., off:off + w])
        off += w
    return outs


def to_heads(t, n_heads):
    b, s, _ = t.shape
    return t.reshape(b, s, n_heads, -1).transpose(0, 2, 1, 3)


def from_heads(t):
    b, h, s, d = t.shape
    return t.transpose(0, 2, 1, 3).reshape(b, s, h * d)


def rope(t):
    s, d = t.shape[2], t.shape[3]
    inv = jnp.power(ROPE_BASE, -jnp.arange(0, d, 2, dtype=jnp.float32) / d)
    ang = jnp.arange(s, dtype=jnp.float32)[:, None] * inv[None, :]
    cos, sin = jnp.cos(ang), jnp.sin(ang)
    tf = t.astype(jnp.float32)
    t1, t2 = tf[..., : d // 2], tf[..., d // 2:]
    return jnp.concatenate([t1 * cos - t2 * sin, t1 * sin + t2 * cos], axis=-1).astype(t.dtype)


def retention_chunkwise(q, k, v):
    in_dtype = v.dtype
    q = q.astype(jnp.float32)
    k = k.astype(jnp.float32) * (q.shape[-1] ** -0.5)
    v = v.astype(jnp.float32)
    b, h, s, dk = q.shape
    dv = v.shape[-1]
    n = s // CHUNK
    log_gamma = jnp.log1p(-jnp.exp2(-5.0 - jnp.arange(h, dtype=jnp.float32)))
    idx = jnp.arange(CHUNK, dtype=jnp.float32)
    diff = idx[:, None] - idx[None, :]
    decay_intra = jnp.where(diff >= 0, jnp.exp(log_gamma[:, None, None] * jnp.maximum(diff, 0.0)), 0.0)
    xi = jnp.exp(log_gamma[:, None] * (idx + 1.0))
    zeta = jnp.exp(log_gamma[:, None] * (CHUNK - 1.0 - idx))
    gamma_c = jnp.exp(log_gamma * CHUNK)

    qc = q.reshape(b, h, n, CHUNK, dk)
    kc = k.reshape(b, h, n, CHUNK, dk)
    vc = v.reshape(b, h, n, CHUNK, dv)

    scores = jnp.einsum('bhncd,bhnsd->bhncs', qc, kc) * decay_intra[None, :, None]
    inner = jnp.einsum('bhncs,bhnse->bhnce', scores, vc)

    kv = jnp.einsum('bhncd,bhnce->bhnde', kc * zeta[None, :, None, :, None], vc)

    def step(state, kv_n):
        return gamma_c[None, :, None, None] * state + kv_n, state

    _, state_prev = lax.scan(step, jnp.zeros((b, h, dk, dv), jnp.float32), jnp.moveaxis(kv, 2, 0))
    state_prev = jnp.moveaxis(state_prev, 0, 2)
    cross = jnp.einsum('bhncd,bhnde->bhnce', qc * xi[None, :, None, :, None], state_prev)
    return (inner + cross).reshape(b, h, s, dv).astype(in_dtype)


def forgetting_attention(q, k, v, log_f):
    b, h, s, d = q.shape
    c = jnp.cumsum(log_f, axis=-1)
    kpos = jnp.arange(s)
    scale = d ** -0.5
    n_blocks = s // Q_BLOCK

    def block(i):
        start = i * Q_BLOCK
        qb = lax.dynamic_slice_in_dim(q, start, Q_BLOCK, axis=2)
        cb = lax.dynamic_slice_in_dim(c, start, Q_BLOCK, axis=2)
        qpos = start + jnp.arange(Q_BLOCK)
        logits = jnp.einsum('bhqd,bhkd->bhqk', qb, k).astype(jnp.float32) * scale
        logits = logits + (cb[..., :, None] - c[..., None, :])
        logits = jnp.where(kpos[None, :] <= qpos[:, None], logits, -1e30)
        p = jax.nn.softmax(logits, axis=-1)
        return jnp.einsum('bhqk,bhkd->bhqd', p.astype(v.dtype), v)

    out = lax.map(block, jnp.arange(n_blocks))
    return jnp.moveaxis(out, 0, 2).reshape(b, h, s, d)


def mixing_sublayer(h, w_in, b_forget, ret_norm, w_o_ret, w_o_fox, w_out):
    proj = h @ w_in
    rq, rk, rv, rg, fq, fk, fv, ff, gate_r, gate_f = split_cols(proj)

    ret = retention_chunkwise(rope(to_heads(rq, RET_HEADS)), rope(to_heads(rk, RET_HEADS)), to_heads(rv, RET_HEADS))
    retf = ret.astype(jnp.float32)
    retf = retf * lax.rsqrt(jnp.mean(retf * retf, axis=-1, keepdims=True) + EPS)
    ret = from_heads(retf.astype(h.dtype)) * ret_norm
    y_ret = (jax.nn.silu(rg) * ret) @ w_o_ret

    log_f = jax.nn.log_sigmoid((ff + b_forget).astype(jnp.float32))
    fox = forgetting_attention(to_heads(fq, FOX_HEADS), to_heads(fk, FOX_HEADS), to_heads(fv, FOX_HEADS),
                               log_f.transpose(0, 2, 1))
    y_fox = from_heads(fox) @ w_o_fox

    merged = jax.nn.sigmoid(gate_r) * y_ret + jax.nn.sigmoid(gate_f) * y_fox
    return merged @ w_out


def setup_inputs(seed: int = 0) -> dict:
    key = jax.random.key(seed)
    ks = jax.random.split(key, 16)
    f32 = jnp.float32

    def nrm(k, shape, fan_in):
        return jax.random.normal(k, shape, f32) * (fan_in ** -0.5)

    def gain(k, shape):
        return 1.0 + 0.02 * jax.random.normal(k, shape, f32)

    return {
        "x": jax.random.normal(ks[0], (BATCH, SEQ, D_MODEL), f32),
        "norm_ffn1": gain(ks[1], (DEPTH, D_MODEL)),
        "w_ffn1_in": nrm(ks[2], (DEPTH, D_MODEL, 2 * D_FF), D_MODEL),
        "w_ffn1_out": nrm(ks[3], (DEPTH, D_FF, D_MODEL), D_FF),
        "norm_mix": gain(ks[4], (DEPTH, D_MODEL)),
        "w_in": nrm(ks[5], (DEPTH, D_MODEL, IN_COLS), D_MODEL),
        "b_forget": jax.random.uniform(ks[6], (DEPTH, FOX_HEADS), f32, minval=1.0, maxval=4.0),
        "ret_norm": gain(ks[7], (DEPTH, RET_V)),
        "w_o_ret": nrm(ks[8], (DEPTH, RET_V, D_MODEL), RET_V),
        "w_o_fox": nrm(ks[9], (DEPTH, FOX_W, D_MODEL), FOX_W),
        "w_out": nrm(ks[10], (DEPTH, D_MODEL, D_MODEL), D_MODEL),
        "norm_ffn2": gain(ks[11], (DEPTH, D_MODEL)),
        "w_ffn2_in": nrm(ks[12], (DEPTH, D_MODEL, 2 * D_FF), D_MODEL),
        "w_ffn2_out": nrm(ks[13], (DEPTH, D_FF, D_MODEL), D_FF),
        "norm_final": gain(ks[14], (D_MODEL,)),
    }


def reference(x, norm_ffn1, w_ffn1_in, w_ffn1_out, norm_mix, w_in, b_forget, ret_norm,
              w_o_ret, w_o_fox, w_out, norm_ffn2, w_ffn2_in, w_ffn2_out, norm_final):
    for l in range(DEPTH):
        x = x + 0.5 * swiglu(rmsnorm(x, norm_ffn1[l]), w_ffn1_in[l], w_ffn1_out[l])
        h = rmsnorm(x, norm_mix[l])
        x = x + mixing_sublayer(h, w_in[l], b_forget[l], ret_norm[l], w_o_ret[l], w_o_fox[l], w_out[l])
        x = x + 0.5 * swiglu(rmsnorm(x, norm_ffn2[l]), w_ffn2_in[l], w_ffn2_out[l])
    return rmsnorm(x, norm_final)
```

```python
import functools

import jax
import jax.numpy as jnp
import numpy as np
from jax import lax
from jax.experimental import pallas as pl
from jax.experimental.pallas import tpu as pltpu

D_MODEL = 1024
DEPTH = 4
RET_HEADS = 4
RET_DK = 128
RET_DV = 128
FOX_HEADS = 8
FOX_DH = 64
D_FF = 2816
ROPE_BASE = 10000.0
EPS = 1e-6

RET_W = RET_HEADS * RET_DK
FOX_W = FOX_HEADS * FOX_DH
FF_OFF = 4 * RET_W + 3 * FOX_W
MAIN_COLS = FF_OFF + 2 * D_MODEL

LANES = 128
MXU_COLS = 256
VMEM_LIMIT_BYTES = 56 * 1024 * 1024

FFN_TM = 512
FFN_CH = MXU_COLS
PROJ_TM = 512
RET_C = 256
FOX_T = 256
OUT_TM = 512
CK_ROWS = 16
NEG = -1e30

BF16 = jnp.bfloat16
F32 = jnp.float32


def _dot(a, b):
    return jnp.dot(a, b, preferred_element_type=F32)


def _dot_nt(a, b):
    return lax.dot_general(a, b, (((1,), (1,)), ((), ())), preferred_element_type=F32)


def _dot_tn(a, b):
    return lax.dot_general(a, b, (((0,), (0,)), ((), ())), preferred_element_type=F32)


def _rms(x, g):
    ms = jnp.mean(x * x, axis=-1, keepdims=True)
    return x * lax.rsqrt(ms + EPS) * g


def _split3(x):
    hi = x.astype(BF16)
    r = x - hi.astype(F32)
    mid = r.astype(BF16)
    lo = (r - mid.astype(F32)).astype(BF16)
    return hi, mid, lo


def _resident(shape):
    nd = len(shape)
    return pl.BlockSpec(shape, lambda *_: (0,) * nd, pipeline_mode=pl.Buffered(1))


def _params(*sem):
    return pltpu.CompilerParams(dimension_semantics=sem, vmem_limit_bytes=VMEM_LIMIT_BYTES)


def _ffn_kernel(x_ref, g_ref, win_ref, wout_ref, gfin_ref, o_ref, acc_ref, *, final_norm):
    x = x_ref[...]
    xn = _rms(x, g_ref[...]).astype(BF16)
    for c in range(D_FF // FFN_CH):
        lo = c * FFN_CH
        a = _dot(xn, win_ref[:, lo:lo + FFN_CH])
        b = _dot(xn, win_ref[:, D_FF + lo:D_FF + lo + FFN_CH])
        h = (a * jax.nn.sigmoid(a) * b).astype(BF16)
        y = _dot(h, wout_ref[lo:lo + FFN_CH, :])
        if c == 0:
            acc_ref[...] = y
        else:
            acc_ref[...] += y
    out = x + 0.5 * acc_ref[...]
    if final_norm:
        out = _rms(out, gfin_ref[...])
    o_ref[...] = out


def _ffn(x, g, w_in, w_out, g_final, final_norm):
    t = x.shape[0]
    tile = pl.BlockSpec((FFN_TM, D_MODEL), lambda i: (i, 0))
    return pl.pallas_call(
        functools.partial(_ffn_kernel, final_norm=final_norm),
        grid=(t // FFN_TM,),
        in_specs=[tile, _resident((1, D_MODEL)), _resident((D_MODEL, 2 * D_FF)),
                  _resident((D_FF, D_MODEL)), _resident((1, D_MODEL))],
        out_specs=tile,
        out_shape=jax.ShapeDtypeStruct((t, D_MODEL), F32),
        scratch_shapes=[pltpu.VMEM((FFN_TM, D_MODEL), F32)],
        compiler_params=_params("arbitrary"),
        name="ffn",
    )(x, g, w_in, w_out, g_final)


def _proj_kernel(x_ref, g_ref, w_ref, wf_ref, wft_ref, bcol_ref, brow_ref, cos_ref, sin_ref,
                 rq_ref, rk_ref, rv_ref, rg_ref, fq_ref, fk_ref, fv_ref, gr_ref, gf_ref,
                 cq_ref, ck_ref, carry_q, carry_k):
    tm = PROJ_TM

    @pl.when(pl.program_id(1) == 0)
    def _():
        carry_q[...] = jnp.zeros_like(carry_q)
        carry_k[...] = jnp.zeros_like(carry_k)

    h = _rms(x_ref[...], g_ref[...]).astype(BF16)

    def cols(off, width):
        return _dot(h, w_ref[:, off:off + width])

    cos = cos_ref[...]
    sin = sin_ref[...]

    def rope(t):
        parts = []
        for hd in range(RET_HEADS):
            th = t[:, hd * RET_DK:(hd + 1) * RET_DK]
            parts.append(th * cos + pltpu.roll(th, RET_DK // 2, 1) * sin)
        return jnp.concatenate(parts, axis=-1)

    rq_ref[...] = rope(cols(0, RET_W)).astype(BF16)
    rk_ref[...] = (rope(cols(RET_W, RET_W)) * (RET_DK ** -0.5)).astype(BF16)
    rv_ref[...] = cols(2 * RET_W, RET_W).astype(BF16)
    rg_ref[...] = cols(3 * RET_W, RET_W)
    fq_ref[...] = (cols(4 * RET_W, FOX_W) * (FOX_DH ** -0.5)).astype(BF16)
    fk_ref[...] = cols(4 * RET_W + FOX_W, FOX_W).astype(BF16)
    fv_ref[...] = cols(4 * RET_W + 2 * FOX_W, FOX_W).astype(BF16)
    gr_ref[...] = jax.nn.sigmoid(cols(FF_OFF, D_MODEL))
    gf_ref[...] = jax.nn.sigmoid(cols(FF_OFF + D_MODEL, D_MODEL))

    ff = _dot(h, wf_ref[...]) + bcol_ref[...]
    logf = jax.nn.log_sigmoid(ff)
    row = lax.broadcasted_iota(jnp.int32, (tm, tm), 0)
    col = lax.broadcasted_iota(jnp.int32, (tm, tm), 1)
    lower = jnp.where(col <= row, 1.0, 0.0).astype(BF16)
    c_q = carry_q[...] + sum(_dot(lower, part) for part in _split3(logf))
    cq_ref[...] = c_q
    carry_q[...] = c_q[tm - 1:tm, :]

    fft = _dot_nt(wft_ref[...], h) + brow_ref[...]
    logft = jax.nn.log_sigmoid(fft)
    upper = jnp.where(row <= col, 1.0, 0.0).astype(BF16)
    c_k = carry_k[:, 0:1] + sum(_dot(part, upper) for part in _split3(logft))
    for s in range(tm // FOX_T):
        ck_ref[0, s] = c_k[:, s * FOX_T:(s + 1) * FOX_T]
    carry_k[...] = jnp.broadcast_to(c_k[:, tm - 1:tm], carry_k.shape)


def _proj(x, g, w_main, wf, wft, bcol, brow, cos, sin, batch, seq):
    t = x.shape[0]
    tm = PROJ_TM
    ns = seq // tm

    def tok(width):
        return pl.BlockSpec((tm, width), lambda b, s: (b * ns + s, 0))

    pos = pl.BlockSpec((tm, RET_DK), lambda b, s: (s, 0))
    out_shapes = [jax.ShapeDtypeStruct((t, RET_W), BF16)] * 3 + [
        jax.ShapeDtypeStruct((t, RET_W), F32)] + [
        jax.ShapeDtypeStruct((t, FOX_W), BF16)] * 3 + [
        jax.ShapeDtypeStruct((t, D_MODEL), F32)] * 2 + [
        jax.ShapeDtypeStruct((t, LANES), F32),
        jax.ShapeDtypeStruct((batch, seq // FOX_T, CK_ROWS, FOX_T), F32)]
    out_specs = [tok(RET_W)] * 4 + [tok(FOX_W)] * 3 + [tok(D_MODEL)] * 2 + [
        tok(LANES),
        pl.BlockSpec((1, tm // FOX_T, CK_ROWS, FOX_T), lambda b, s: (b, s, 0, 0))]
    return pl.pallas_call(
        _proj_kernel,
        grid=(batch, ns),
        in_specs=[tok(D_MODEL), _resident((1, D_MODEL)), _resident((D_MODEL, MAIN_COLS)),
                  _resident((D_MODEL, LANES)), _resident((CK_ROWS, D_MODEL)),
                  _resident((1, LANES)), _resident((CK_ROWS, 1)), pos, pos],
        out_specs=out_specs,
        out_shape=out_shapes,
        scratch_shapes=[pltpu.VMEM((1, LANES), F32), pltpu.VMEM((CK_ROWS, LANES), F32)],
        compiler_params=_params("arbitrary", "arbitrary"),
        name="mix_proj",
    )(x, g, w_main, wf, wft, bcol, brow, cos, sin)


def _ret_kernel(q_ref, k_ref, v_ref, rg_ref, rn_ref, o_ref, state_ref):
    c = RET_C

    @pl.when(pl.program_id(1) == 0)
    def _():
        state_ref[...] = jnp.zeros_like(state_ref)

    row = lax.broadcasted_iota(jnp.int32, (c, c), 0)
    col = lax.broadcasted_iota(jnp.int32, (c, c), 1)
    diff = (row - col).astype(F32)
    pos = lax.broadcasted_iota(jnp.int32, (c, 1), 0).astype(F32)
    for hd in range(RET_HEADS):
        log_gamma = float(np.log1p(-np.exp2(-5.0 - hd)))
        sl = slice(hd * RET_DK, (hd + 1) * RET_DK)
        q = q_ref[:, sl]
        k = k_ref[:, sl]
        v = v_ref[:, sl]
        decay = jnp.where(diff >= 0, jnp.exp(log_gamma * jnp.maximum(diff, 0.0)), 0.0)
        scores = _dot_nt(q, k) * decay
        inner = _dot(scores.astype(BF16), v)
        state = state_ref[hd]
        xi = jnp.exp(log_gamma * (pos + 1.0))
        cross = _dot(q, state.astype(BF16)) * xi
        zeta = jnp.exp(log_gamma * (c - 1.0 - pos))
        vz = (v.astype(F32) * zeta).astype(BF16)
        state_ref[hd] = float(np.exp(log_gamma * c)) * state + _dot_tn(k, vz)
        ret = inner + cross
        ret = ret * lax.rsqrt(jnp.mean(ret * ret, axis=-1, keepdims=True) + EPS)
        o_ref[:, sl] = (jax.nn.silu(rg_ref[:, sl]) * (ret * rn_ref[:, sl])).astype(BF16)


def _retention(rq, rk, rv, rg, ret_norm, batch, seq):
    t = rq.shape[0]
    nc = seq // RET_C
    tile = pl.BlockSpec((RET_C, RET_W), lambda b, s: (b * nc + s, 0))
    return pl.pallas_call(
        _ret_kernel,
        grid=(batch, nc),
        in_specs=[tile, tile, tile, tile, _resident((1, RET_W))],
        out_specs=tile,
        out_shape=jax.ShapeDtypeStruct((t, RET_W), BF16),
        scratch_shapes=[pltpu.VMEM((RET_HEADS, RET_DK, RET_DV), F32)],
        compiler_params=_params("arbitrary", "arbitrary"),
        name="retention",
    )(rq, rk, rv, rg, ret_norm)


def _fox_kernel(q_ref, k_ref, v_ref, cq_ref, ck_ref, o_ref, m_ref, l_ref, acc_ref):
    t = FOX_T
    hp = pl.program_id(1)
    qi = pl.program_id(2)
    lane = lax.broadcasted_iota(jnp.int32, (t, LANES), 1)
    q2 = q_ref[...]
    zero = jnp.zeros_like(q2)
    cq_all = cq_ref[...]
    q_heads = []
    cq_heads = []
    for j in range(2):
        q_heads.append(jnp.where((lane >= j * FOX_DH) & (lane < (j + 1) * FOX_DH), q2, zero))
        cq_heads.append(jnp.sum(jnp.where(lane == 2 * hp + j, cq_all, 0.0), axis=-1, keepdims=True))
    m_ref[...] = jnp.full_like(m_ref, NEG)
    l_ref[...] = jnp.zeros_like(l_ref)
    acc_ref[...] = jnp.zeros_like(acc_ref)

    def block(kv, masked):
        k2 = k_ref[pl.ds(pl.multiple_of(kv * t, t), t), :]
        v2 = v_ref[pl.ds(pl.multiple_of(kv * t, t), t), :]
        for j in range(2):
            ck = ck_ref[0, kv, pl.ds(2 * hp + j, 1), :]
            s = _dot_nt(q_heads[j], k2) + (cq_heads[j] - ck)
            if masked:
                r = lax.broadcasted_iota(jnp.int32, (t, t), 0)
                cidx = lax.broadcasted_iota(jnp.int32, (t, t), 1)
                s = jnp.where(cidx <= r, s, NEG)
            m_old = m_ref[j]
            m_new = jnp.maximum(m_old, jnp.max(s, axis=-1, keepdims=True))
            alpha = jnp.exp(m_old - m_new)
            p = jnp.exp(s - m_new)
            l_ref[j] = alpha * l_ref[j] + jnp.sum(p, axis=-1, keepdims=True)
            acc_ref[j] = alpha * acc_ref[j] + _dot(p.astype(BF16), v2)
            m_ref[j] = m_new

    def body(kv, carry):
        block(kv, False)
        return carry

    lax.fori_loop(0, qi, body, 0)
    block(qi, True)
    out = [acc_ref[j] / l_ref[j] for j in range(2)]
    o_ref[...] = jnp.where(lane < FOX_DH, out[0], out[1]).astype(BF16)


def _fox(fq, fk, fv, cq, ck, batch, seq):
    t = fq.shape[0]
    nq = seq // FOX_T
    npair = FOX_W // LANES
    return pl.pallas_call(
        _fox_kernel,
        grid=(batch, npair, nq),
        in_specs=[
            pl.BlockSpec((FOX_T, LANES), lambda b, p, i: (b * nq + i, p)),
            pl.BlockSpec((seq, LANES), lambda b, p, i: (b, p)),
            pl.BlockSpec((seq, LANES), lambda b, p, i: (b, p)),
            pl.BlockSpec((FOX_T, LANES), lambda b, p, i: (b * nq + i, 0)),
            pl.BlockSpec((1, nq, CK_ROWS, FOX_T), lambda b, p, i: (b, 0, 0, 0)),
        ],
        out_specs=pl.BlockSpec((FOX_T, LANES), lambda b, p, i: (b * nq + i, p)),
        out_shape=jax.ShapeDtypeStruct((t, FOX_W), BF16),
        scratch_shapes=[pltpu.VMEM((2, FOX_T, 1), F32), pltpu.VMEM((2, FOX_T, 1), F32),
                        pltpu.VMEM((2, FOX_T, LANES), F32)],
        compiler_params=_params("arbitrary", "arbitrary", "arbitrary"),
        name="fox_attn",
    )(fq, fk, fv, cq, ck)


def _mixout_kernel(x_ref, ret_ref, fox_ref, gr_ref, gf_ref, wor_ref, wof_ref, wout_ref, o_ref):
    y_ret = _dot(ret_ref[...], wor_ref[...])
    y_fox = _dot(fox_ref[...], wof_ref[...])
    merged = gr_ref[...] * y_ret + gf_ref[...] * y_fox
    o_ref[...] = x_ref[...] + _dot(merged.astype(BF16), wout_ref[...])


def _mixout(x, ret, fox, gr, gf, w_o_ret, w_o_fox, w_out):
    t = x.shape[0]

    def tok(width):
        return pl.BlockSpec((OUT_TM, width), lambda i: (i, 0))

    return pl.pallas_call(
        _mixout_kernel,
        grid=(t // OUT_TM,),
        in_specs=[tok(D_MODEL), tok(RET_W), tok(FOX_W), tok(D_MODEL), tok(D_MODEL),
                  _resident((RET_W, D_MODEL)), _resident((FOX_W, D_MODEL)),
                  _resident((D_MODEL, D_MODEL))],
        out_specs=tok(D_MODEL),
        out_shape=jax.ShapeDtypeStruct((t, D_MODEL), F32),
        compiler_params=_params("arbitrary"),
        name="mix_out",
    )(x, ret, fox, gr, gf, w_o_ret, w_o_fox, w_out)


def _rope_tables(seq):
    d = RET_DK
    inv = jnp.power(ROPE_BASE, -jnp.arange(0, d, 2, dtype=F32) / d)
    ang = jnp.arange(seq, dtype=F32)[:, None] * inv[None, :]
    cos, sin = jnp.cos(ang), jnp.sin(ang)
    return jnp.concatenate([cos, cos], axis=-1), jnp.concatenate([-sin, sin], axis=-1)


def kernel(x, norm_ffn1, w_ffn1_in, w_ffn1_out, norm_mix, w_in, b_forget, ret_norm,
           w_o_ret, w_o_fox, w_out, norm_ffn2, w_ffn2_in, w_ffn2_out, norm_final):
    batch, seq, d = x.shape
    assert d == D_MODEL and seq % PROJ_TM == 0 and (batch * seq) % FFN_TM == 0
    xt = x.reshape(batch * seq, d)

    w1i = w_ffn1_in.astype(BF16)
    w1o = w_ffn1_out.astype(BF16)
    w2i = w_ffn2_in.astype(BF16)
    w2o = w_ffn2_out.astype(BF16)
    w_main = jnp.concatenate([w_in[..., :FF_OFF], w_in[..., FF_OFF + FOX_HEADS:]], axis=-1).astype(BF16)
    w_f = w_in[..., FF_OFF:FF_OFF + FOX_HEADS]
    wf = jnp.pad(w_f, ((0, 0), (0, 0), (0, LANES - FOX_HEADS))).astype(BF16)
    wft = jnp.pad(jnp.swapaxes(w_f, 1, 2), ((0, 0), (0, CK_ROWS - FOX_HEADS), (0, 0))).astype(BF16)
    bcol = jnp.pad(b_forget, ((0, 0), (0, LANES - FOX_HEADS)))[:, None, :]
    brow = jnp.pad(b_forget, ((0, 0), (0, CK_ROWS - FOX_HEADS)))[:, :, None]
    wor = w_o_ret.astype(BF16)
    wof = w_o_fox.astype(BF16)
    wo = w_out.astype(BF16)
    cos, sin = _rope_tables(seq)
    g_final = norm_final[None, :]

    for l in range(DEPTH):
        xt = _ffn(xt, norm_ffn1[l][None, :], w1i[l], w1o[l], g_final, False)
        rq, rk, rv, rg, fq, fk, fv, gr, gf, cq, ck = _proj(
            xt, norm_mix[l][None, :], w_main[l], wf[l], wft[l], bcol[l], brow[l], cos, sin, batch, seq)
        ret = _retention(rq, rk, rv, rg, ret_norm[l][None, :], batch, seq)
        fox = _fox(fq, fk, fv, cq, ck, batch, seq)
        xt = _mixout(xt, ret, fox, gr, gf, wor[l], wof[l], wo[l])
        xt = _ffn(xt, norm_ffn2[l][None, :], w2i[l], w2o[l], g_final, l == DEPTH - 1)
    return xt.reshape(batch, seq, d)
```

```python
import functools
import math

import jax
import jax.numpy as jnp
import numpy as np
from jax import lax
from jax.experimental import pallas as pl
from jax.experimental.pallas import tpu as pltpu

D_MODEL = 1024
DEPTH = 4
RET_HEADS = 4
RET_DK = 128
RET_DV = 128
FOX_HEADS = 8
FOX_DH = 64
D_FF = 2816
ROPE_BASE = 10000.0
EPS = 1e-6

RET_W = RET_HEADS * RET_DK
FOX_W = FOX_HEADS * FOX_DH
FF_OFF = 4 * RET_W + 3 * FOX_W
MAIN_COLS = 5 * RET_W + 2 * D_MODEL
GATE_OFF = 5 * RET_W

LANES = 128
MXU_COLS = 256
BF16_ROWS = 16
VMEM_LIMIT_BYTES = 56 * 1024 * 1024

FFN_TM = 512
FFN_CH = MXU_COLS
PROJ_TM = 512
RET_C = 256
FOX_T = 256
FOX_GROUP = 8
OUT_TM = 512
NEG = -1e30
LOG2E = math.log2(math.e)

AUG_PARTS = 3
AUG_SPAN = 2 * AUG_PARTS
ONES_SLOT = FOX_HEADS

BF16 = jnp.bfloat16
F32 = jnp.float32


def _dot(a, b):
    return jnp.dot(a, b, preferred_element_type=F32)


def _dot_nt(a, b):
    return lax.dot_general(a, b, (((1,), (1,)), ((), ())), preferred_element_type=F32)


def _dot_tn(a, b):
    return lax.dot_general(a, b, (((0,), (0,)), ((), ())), preferred_element_type=F32)


def _rms(x, g):
    ms = jnp.mean(x * x, axis=-1, keepdims=True)
    return x * lax.rsqrt(ms + EPS) * g


def _split3(x):
    hi = x.astype(BF16)
    r = x - hi.astype(F32)
    mid = r.astype(BF16)
    lo = (r - mid.astype(F32)).astype(BF16)
    return hi, mid, lo


def _resident(shape):
    nd = len(shape)
    return pl.BlockSpec(shape, lambda *_: (0,) * nd, pipeline_mode=pl.Buffered(1))


def _params(*sem):
    return pltpu.CompilerParams(dimension_semantics=sem, vmem_limit_bytes=VMEM_LIMIT_BYTES)


def _aug_base(head):
    return BF16_ROWS * (head // 2) + AUG_SPAN * (head % 2)


def _aug_selectors():
    sel_k = np.zeros((LANES, LANES), np.float32)
    sel_q = np.zeros((LANES, LANES), np.float32)
    for h in range(FOX_HEADS):
        base = _aug_base(h)
        for part in range(AUG_PARTS):
            sel_q[base + part, BF16_ROWS * part + h] = 1.0
            sel_q[base + AUG_PARTS + part, ONES_SLOT] = 1.0
            sel_k[AUG_PARTS * FOX_HEADS, base + part] = 1.0
            sel_k[FOX_HEADS * part + h, base + AUG_PARTS + part] = -1.0
    return jnp.asarray(sel_k, BF16), jnp.asarray(sel_q, BF16)


def _ffn_kernel(x_ref, g_ref, win_ref, wout_ref, gfin_ref, o_ref, acc_ref, *, final_norm):
    x = x_ref[...]
    xn = _rms(x, g_ref[...]).astype(BF16)
    for c in range(D_FF // FFN_CH):
        lo = c * FFN_CH
        a = _dot(xn, win_ref[:, lo:lo + FFN_CH])
        b = _dot(xn, win_ref[:, D_FF + lo:D_FF + lo + FFN_CH])
        h = (a * jax.nn.sigmoid(a) * b).astype(BF16)
        y = _dot(h, wout_ref[lo:lo + FFN_CH, :])
        if c == 0:
            acc_ref[...] = y
        else:
            acc_ref[...] += y
    out = x + 0.5 * acc_ref[...]
    if final_norm:
        out = _rms(out, gfin_ref[...])
    o_ref[...] = out


def _ffn(x, g, w_in, w_out, g_final, final_norm):
    t = x.shape[0]
    tile = pl.BlockSpec((FFN_TM, D_MODEL), lambda i: (i, 0))
    return pl.pallas_call(
        functools.partial(_ffn_kernel, final_norm=final_norm),
        grid=(t // FFN_TM,),
        in_specs=[tile, _resident((1, D_MODEL)), _resident((D_MODEL, 2 * D_FF)),
                  _resident((D_FF, D_MODEL)), _resident((1, D_MODEL))],
        out_specs=tile,
        out_shape=jax.ShapeDtypeStruct((t, D_MODEL), F32),
        scratch_shapes=[pltpu.VMEM((FFN_TM, D_MODEL), F32)],
        compiler_params=_params("arbitrary"),
        name="ffn",
    )(x, g, w_in, w_out, g_final)


def _proj_kernel(x_ref, g_ref, w_ref, wqv_ref, wf_ref, wft_ref, bcol_ref, brow_ref,
                 selk_ref, selq_ref, cos_ref, sin_ref,
                 rq_ref, rk_ref, rv_ref, rg_ref, fk_ref, gr_ref, gf_ref,
                 fqt_ref, fvt_ref, kaug_ref, qaug_ref, carry_q, carry_k):
    tm = PROJ_TM

    @pl.when(pl.program_id(1) == 0)
    def _():
        carry_q[...] = jnp.zeros_like(carry_q)
        carry_k[...] = jnp.zeros_like(carry_k)

    h = _rms(x_ref[...], g_ref[...]).astype(BF16)

    def cols(off, width):
        return _dot(h, w_ref[:, off:off + width])

    cos = cos_ref[...]
    sin = sin_ref[...]

    def rope(t):
        parts = []
        for hd in range(RET_HEADS):
            th = t[:, hd * RET_DK:(hd + 1) * RET_DK]
            parts.append(th * cos + pltpu.roll(th, RET_DK // 2, 1) * sin)
        return jnp.concatenate(parts, axis=-1)

    rq_ref[...] = rope(cols(0, RET_W)).astype(BF16)
    rk_ref[...] = (rope(cols(RET_W, RET_W)) * (RET_DK ** -0.5)).astype(BF16)
    rv_ref[...] = cols(2 * RET_W, RET_W).astype(BF16)
    rg_ref[...] = cols(3 * RET_W, RET_W)
    fk_ref[...] = cols(4 * RET_W, FOX_W).astype(BF16)
    gr_ref[...] = jax.nn.sigmoid(cols(GATE_OFF, D_MODEL))
    gf_ref[...] = jax.nn.sigmoid(cols(GATE_OFF + D_MODEL, D_MODEL))

    fqt_ref[0] = (_dot_nt(wqv_ref[0:FOX_W, :], h) * (FOX_DH ** -0.5 * LOG2E)).astype(BF16)
    fvt_ref[0] = _dot_nt(wqv_ref[FOX_W:2 * FOX_W, :], h).astype(BF16)

    row = lax.broadcasted_iota(jnp.int32, (tm, tm), 0)
    col = lax.broadcasted_iota(jnp.int32, (tm, tm), 1)

    logf = jax.nn.log_sigmoid(_dot(h, wf_ref[...]) + bcol_ref[...])
    lower = jnp.where(col <= row, 1.0, 0.0).astype(BF16)
    c_k = carry_k[...] + sum(_dot(lower, part) for part in _split3(logf))
    carry_k[...] = c_k[tm - 1:tm, :]
    hi, mid, lo = _split3(c_k * LOG2E)
    lane = lax.broadcasted_iota(jnp.int32, (tm, LANES), 1)
    packed_k = jnp.where(lane < FOX_HEADS, hi, jnp.where(lane < 2 * FOX_HEADS, mid, lo))
    packed_k = jnp.where(lane == AUG_PARTS * FOX_HEADS, jnp.ones_like(packed_k), packed_k)
    kaug_ref[...] = _dot(packed_k, selk_ref[...]).astype(BF16)

    logft = jax.nn.log_sigmoid(_dot_nt(wft_ref[...], h) + brow_ref[...])
    upper = jnp.where(row <= col, 1.0, 0.0).astype(BF16)
    c_q = carry_q[:, 0:1] + sum(_dot(part, upper) for part in _split3(logft))
    carry_q[...] = jnp.broadcast_to(c_q[:, tm - 1:tm], carry_q.shape)
    hi, mid, lo = _split3(c_q * LOG2E)
    srow = lax.broadcasted_iota(jnp.int32, (BF16_ROWS, tm), 0)
    hi = jnp.where(srow == ONES_SLOT, jnp.ones_like(hi), hi)
    pad = jnp.zeros((LANES - AUG_PARTS * BF16_ROWS, tm), BF16)
    packed_q = jnp.concatenate([hi, mid, lo, pad], axis=0)
    qaug_ref[0] = _dot(selq_ref[...], packed_q).astype(BF16)


def _proj(x, g, w_main, wqv_t, wf, wft, bcol, brow, sel_k, sel_q, cos, sin, batch, seq):
    t = x.shape[0]
    tm = PROJ_TM
    ns = seq // tm

    def tok(width):
        return pl.BlockSpec((tm, width), lambda b, s: (b * ns + s, 0))

    def feat(rows):
        return pl.BlockSpec((1, rows, tm), lambda b, s: (b, 0, s))

    pos = pl.BlockSpec((tm, RET_DK), lambda b, s: (s, 0))
    out_shapes = [jax.ShapeDtypeStruct((t, RET_W), BF16)] * 3 + [
        jax.ShapeDtypeStruct((t, RET_W), F32),
        jax.ShapeDtypeStruct((t, FOX_W), BF16)] + [
        jax.ShapeDtypeStruct((t, D_MODEL), F32)] * 2 + [
        jax.ShapeDtypeStruct((batch, FOX_W, seq), BF16)] * 2 + [
        jax.ShapeDtypeStruct((t, LANES), BF16),
        jax.ShapeDtypeStruct((batch, LANES, seq), BF16)]
    out_specs = [tok(RET_W)] * 4 + [tok(FOX_W)] + [tok(D_MODEL)] * 2 + [
        feat(FOX_W), feat(FOX_W), tok(LANES), feat(LANES)]
    return pl.pallas_call(
        _proj_kernel,
        grid=(batch, ns),
        in_specs=[tok(D_MODEL), _resident((1, D_MODEL)), _resident((D_MODEL, MAIN_COLS)),
                  _resident((2 * FOX_W, D_MODEL)),
                  _resident((D_MODEL, LANES)), _resident((BF16_ROWS, D_MODEL)),
                  _resident((1, LANES)), _resident((BF16_ROWS, 1)),
                  _resident((LANES, LANES)), _resident((LANES, LANES)), pos, pos],
        out_specs=out_specs,
        out_shape=out_shapes,
        scratch_shapes=[pltpu.VMEM((BF16_ROWS, LANES), F32), pltpu.VMEM((1, LANES), F32)],
        compiler_params=_params("arbitrary", "arbitrary"),
        name="mix_proj",
    )(x, g, w_main, wqv_t, wf, wft, bcol, brow, sel_k, sel_q, cos, sin)


def _ret_kernel(q_ref, k_ref, v_ref, rg_ref, rn_ref, o_ref, state_ref):
    c = RET_C

    @pl.when(pl.program_id(1) == 0)
    def _():
        state_ref[...] = jnp.zeros_like(state_ref)

    row = lax.broadcasted_iota(jnp.int32, (c, c), 0)
    col = lax.broadcasted_iota(jnp.int32, (c, c), 1)
    diff = (row - col).astype(F32)
    pos = lax.broadcasted_iota(jnp.int32, (c, 1), 0).astype(F32)
    for hd in range(RET_HEADS):
        log_gamma = float(np.log1p(-np.exp2(-5.0 - hd)))
        sl = slice(hd * RET_DK, (hd + 1) * RET_DK)
        q = q_ref[:, sl]
        k = k_ref[:, sl]
        v = v_ref[:, sl]
        decay = jnp.where(diff >= 0, jnp.exp(log_gamma * jnp.maximum(diff, 0.0)), 0.0)
        scores = _dot_nt(q, k) * decay
        inner = _dot(scores.astype(BF16), v)
        state = state_ref[hd]
        xi = jnp.exp(log_gamma * (pos + 1.0))
        cross = _dot(q, state.astype(BF16)) * xi
        zeta = jnp.exp(log_gamma * (c - 1.0 - pos))
        vz = (v.astype(F32) * zeta).astype(BF16)
        state_ref[hd] = float(np.exp(log_gamma * c)) * state + _dot_tn(k, vz)
        ret = inner + cross
        ret = ret * lax.rsqrt(jnp.mean(ret * ret, axis=-1, keepdims=True) + EPS)
        o_ref[:, sl] = (jax.nn.silu(rg_ref[:, sl]) * (ret * rn_ref[:, sl])).astype(BF16)


def _retention(rq, rk, rv, rg, ret_norm, batch, seq):
    t = rq.shape[0]
    nc = seq // RET_C
    tile = pl.BlockSpec((RET_C, RET_W), lambda b, s: (b * nc + s, 0))
    return pl.pallas_call(
        _ret_kernel,
        grid=(batch, nc),
        in_specs=[tile, tile, tile, tile, _resident((1, RET_W))],
        out_specs=tile,
        out_shape=jax.ShapeDtypeStruct((t, RET_W), BF16),
        scratch_shapes=[pltpu.VMEM((RET_HEADS, RET_DK, RET_DV), F32)],
        compiler_params=_params("arbitrary", "arbitrary"),
        name="retention",
    )(rq, rk, rv, rg, ret_norm)


def _fox_kernel(qt_ref, qaug_ref, k_ref, kaug_ref, vt_ref, o_ref,
                w_ref, s_ref, s2_ref, p_ref, a_ref, m_ref, l_ref, acc_ref):
    t = FOX_T
    first_head = pl.program_id(1) * FOX_GROUP
    qi = pl.program_id(2)
    rowi = lax.broadcasted_iota(jnp.int32, (LANES, t), 0)
    qaug = qaug_ref[0]
    for j in range(FOX_GROUP):
        qpair = qt_ref[0, (j // 2) * LANES:(j // 2 + 1) * LANES, :]
        zero = jnp.zeros_like(qpair)
        lo = (first_head + j) // 2 * BF16_ROWS + (j % 2) * AUG_SPAN
        top = jnp.where((rowi >= (j % 2) * FOX_DH) & (rowi < (j % 2 + 1) * FOX_DH), qpair, zero)
        bot = jnp.where((rowi >= lo) & (rowi < lo + AUG_SPAN), qaug, zero)
        w_ref[j, 0:LANES, :] = top
        w_ref[j, LANES:2 * LANES, :] = bot
    m_ref[...] = jnp.full_like(m_ref, NEG)
    l_ref[...] = jnp.zeros_like(l_ref)
    acc_ref[...] = jnp.zeros_like(acc_ref)
    p_ref[...] = jnp.zeros_like(p_ref)
    a_ref[...] = jnp.ones_like(a_ref)

    def rows_of(kv):
        return pl.ds(pl.multiple_of(kv * t, t), t)

    def score_stage(kv, dst_ref):
        rows = rows_of(kv)
        kaug = kaug_ref[rows, :]
        for j in range(FOX_GROUP):
            kblk = jnp.concatenate([k_ref[rows, (j // 2) * LANES:(j // 2 + 1) * LANES], kaug], axis=1)
            dst_ref[j] = _dot(kblk, w_ref[j])

    def softmax_stage(src_ref, masked):
        for j in range(FOX_GROUP):
            s = src_ref[j]
            if masked:
                r = lax.broadcasted_iota(jnp.int32, (t, t), 0)
                cidx = lax.broadcasted_iota(jnp.int32, (t, t), 1)
                s = jnp.where(r <= cidx, s, NEG)
            m_old = m_ref[j]
            m_new = jnp.maximum(m_old, jnp.max(s, axis=0, keepdims=True))
            alpha = jnp.exp2(m_old - m_new)
            p = jnp.exp2(s - m_new)
            l_ref[j] = alpha * l_ref[j] + jnp.sum(p, axis=0, keepdims=True)
            m_ref[j] = m_new
            a_ref[j] = alpha
            p_ref[j] = p.astype(BF16)

    def value_stage(kv):
        rows = rows_of(kv)
        for j in range(FOX_GROUP):
            vt = vt_ref[0, j * FOX_DH:(j + 1) * FOX_DH, rows]
            acc_ref[j] = a_ref[j] * acc_ref[j] + _dot(vt, p_ref[j])

    def step(i, src_ref, dst_ref):
        value_stage(jnp.maximum(i - 1, 0))
        softmax_stage(src_ref, False)
        score_stage(i + 1, dst_ref)

    def body(h, carry):
        step(2 * h, s_ref, s2_ref)
        step(2 * h + 1, s2_ref, s_ref)
        return carry

    score_stage(0, s_ref)
    lax.fori_loop(0, qi // 2, body, 0)

    @pl.when(qi % 2 == 1)
    def _():
        step(qi - 1, s_ref, s_ref)

    value_stage(jnp.maximum(qi - 1, 0))
    softmax_stage(s_ref, True)
    value_stage(qi)
    out_t = jnp.concatenate([acc_ref[j] / l_ref[j] for j in range(FOX_GROUP)], axis=0)
    o_ref[...] = out_t.T.astype(BF16)


def _fox(fqt, qaug, fk, kaug, fvt, batch, seq):
    t = fk.shape[0]
    nq = seq // FOX_T
    gw = FOX_GROUP * FOX_DH
    return pl.pallas_call(
        _fox_kernel,
        grid=(batch, FOX_HEADS // FOX_GROUP, nq),
        in_specs=[
            pl.BlockSpec((1, gw, FOX_T), lambda b, g, i: (b, g, i)),
            pl.BlockSpec((1, LANES, FOX_T), lambda b, g, i: (b, 0, i)),
            pl.BlockSpec((seq, gw), lambda b, g, i: (b, g)),
            pl.BlockSpec((seq, LANES), lambda b, g, i: (b, 0)),
            pl.BlockSpec((1, gw, seq), lambda b, g, i: (b, g, 0)),
        ],
        out_specs=pl.BlockSpec((FOX_T, gw), lambda b, g, i: (b * nq + i, g)),
        out_shape=jax.ShapeDtypeStruct((t, FOX_W), BF16),
        scratch_shapes=[pltpu.VMEM((FOX_GROUP, 2 * LANES, FOX_T), BF16),
                        pltpu.VMEM((FOX_GROUP, FOX_T, FOX_T), F32),
                        pltpu.VMEM((FOX_GROUP, FOX_T, FOX_T), F32),
                        pltpu.VMEM((FOX_GROUP, FOX_T, FOX_T), BF16),
                        pltpu.VMEM((FOX_GROUP, 1, FOX_T), F32),
                        pltpu.VMEM((FOX_GROUP, 1, FOX_T), F32),
                        pltpu.VMEM((FOX_GROUP, 1, FOX_T), F32),
                        pltpu.VMEM((FOX_GROUP, FOX_DH, FOX_T), F32)],
        compiler_params=_params("arbitrary", "arbitrary", "arbitrary"),
        name="fox_attn",
    )(fqt, qaug, fk, kaug, fvt)


def _mixout_kernel(x_ref, ret_ref, fox_ref, gr_ref, gf_ref, wor_ref, wof_ref, wout_ref, o_ref):
    y_ret = _dot(ret_ref[...], wor_ref[...])
    y_fox = _dot(fox_ref[...], wof_ref[...])
    merged = gr_ref[...] * y_ret + gf_ref[...] * y_fox
    o_ref[...] = x_ref[...] + _dot(merged.astype(BF16), wout_ref[...])


def _mixout(x, ret, fox, gr, gf, w_o_ret, w_o_fox, w_out):
    t = x.shape[0]

    def tok(width):
        return pl.BlockSpec((OUT_TM, width), lambda i: (i, 0))

    return pl.pallas_call(
        _mixout_kernel,
        grid=(t // OUT_TM,),
        in_specs=[tok(D_MODEL), tok(RET_W), tok(FOX_W), tok(D_MODEL), tok(D_MODEL),
                  _resident((RET_W, D_MODEL)), _resident((FOX_W, D_MODEL)),
                  _resident((D_MODEL, D_MODEL))],
        out_specs=tok(D_MODEL),
        out_shape=jax.ShapeDtypeStruct((t, D_MODEL), F32),
        compiler_params=_params("arbitrary"),
        name="mix_out",
    )(x, ret, fox, gr, gf, w_o_ret, w_o_fox, w_out)


def _rope_tables(seq):
    d = RET_DK
    inv = jnp.power(ROPE_BASE, -jnp.arange(0, d, 2, dtype=F32) / d)
    ang = jnp.arange(seq, dtype=F32)[:, None] * inv[None, :]
    cos, sin = jnp.cos(ang), jnp.sin(ang)
    return jnp.concatenate([cos, cos], axis=-1), jnp.concatenate([-sin, sin], axis=-1)


def kernel(x, norm_ffn1, w_ffn1_in, w_ffn1_out, norm_mix, w_in, b_forget, ret_norm,
           w_o_ret, w_o_fox, w_out, norm_ffn2, w_ffn2_in, w_ffn2_out, norm_final):
    batch, seq, d = x.shape
    assert d == D_MODEL and seq % PROJ_TM == 0 and (batch * seq) % FFN_TM == 0
    xt = x.reshape(batch * seq, d)

    w1i = w_ffn1_in.astype(BF16)
    w1o = w_ffn1_out.astype(BF16)
    w2i = w_ffn2_in.astype(BF16)
    w2o = w_ffn2_out.astype(BF16)
    fq_off = 4 * RET_W
    w_main = jnp.concatenate(
        [w_in[..., :fq_off], w_in[..., fq_off + FOX_W:fq_off + 2 * FOX_W],
         w_in[..., FF_OFF + FOX_HEADS:]], axis=-1).astype(BF16)
    wqv_t = jnp.swapaxes(jnp.concatenate(
        [w_in[..., fq_off:fq_off + FOX_W], w_in[..., fq_off + 2 * FOX_W:FF_OFF]], axis=-1), 1, 2).astype(BF16)
    w_f = w_in[..., FF_OFF:FF_OFF + FOX_HEADS]
    wf = jnp.pad(jnp.tile(w_f, (1, 1, AUG_PARTS)),
                 ((0, 0), (0, 0), (0, LANES - AUG_PARTS * FOX_HEADS))).astype(BF16)
    wft = jnp.pad(jnp.swapaxes(w_f, 1, 2), ((0, 0), (0, BF16_ROWS - FOX_HEADS), (0, 0))).astype(BF16)
    bcol = jnp.pad(jnp.tile(b_forget, (1, AUG_PARTS)),
                   ((0, 0), (0, LANES - AUG_PARTS * FOX_HEADS)))[:, None, :]
    brow = jnp.pad(b_forget, ((0, 0), (0, BF16_ROWS - FOX_HEADS)))[:, :, None]
    wor = w_o_ret.astype(BF16)
    wof = w_o_fox.astype(BF16)
    wo = w_out.astype(BF16)
    cos, sin = _rope_tables(seq)
    sel_k, sel_q = _aug_selectors()
    g_final = norm_final[None, :]

    for l in range(DEPTH):
        xt = _ffn(xt, norm_ffn1[l][None, :], w1i[l], w1o[l], g_final, False)
        rq, rk, rv, rg, fk, gr, gf, fqt, fvt, kaug, qaug = _proj(
            xt, norm_mix[l][None, :], w_main[l], wqv_t[l], wf[l], wft[l], bcol[l], brow[l],
            sel_k, sel_q, cos, sin, batch, seq)
        ret = _retention(rq, rk, rv, rg, ret_norm[l][None, :], batch, seq)
        fox = _fox(fqt, qaug, fk, kaug, fvt, batch, seq)
        xt = _mixout(xt, ret, fox, gr, gf, wor[l], wof[l], wo[l])
        xt = _ffn(xt, norm_ffn2[l][None, :], w2i[l], w2o[l], g_final, l == DEPTH - 1)
    return xt.reshape(batch, seq, d)
```

```python
import functools
import math

import jax
import jax.numpy as jnp
import numpy as np
from jax import lax
from jax.experimental import pallas as pl
from jax.experimental.pallas import tpu as pltpu

D_MODEL = 1024
DEPTH = 4
RET_HEADS = 4
RET_DK = 128
RET_DV = 128
FOX_HEADS = 8
FOX_DH = 64
D_FF = 2816
ROPE_BASE = 10000.0
EPS = 1e-6

RET_W = RET_HEADS * RET_DK
FOX_W = FOX_HEADS * FOX_DH
FF_OFF = 4 * RET_W + 3 * FOX_W
MAIN_COLS = 5 * RET_W + 2 * D_MODEL
GATE_OFF = 5 * RET_W

LANES = 128
MXU_COLS = 256
BF16_ROWS = 16
VMEM_LIMIT_BYTES = 56 * 1024 * 1024

FFN_TM = 512
FFN_CH = MXU_COLS
PROJ_TM = 512
RET_C = 256
FOX_T = 256
FOX_GROUP = 8
OUT_TM = 512
NEG = -1e30
LOG2E = math.log2(math.e)

AUG_PARTS = 3
AUG_SPAN = 2 * AUG_PARTS
ONES_SLOT = FOX_HEADS

BF16 = jnp.bfloat16
F32 = jnp.float32


def _dot(a, b):
    return jnp.dot(a, b, preferred_element_type=F32)


def _dot_nt(a, b):
    return lax.dot_general(a, b, (((1,), (1,)), ((), ())), preferred_element_type=F32)


def _dot_tn(a, b):
    return lax.dot_general(a, b, (((0,), (0,)), ((), ())), preferred_element_type=F32)


def _rms(x, g):
    ms = jnp.mean(x * x, axis=-1, keepdims=True)
    return x * lax.rsqrt(ms + EPS) * g


def _split3(x):
    hi = x.astype(BF16)
    r = x - hi.astype(F32)
    mid = r.astype(BF16)
    lo = (r - mid.astype(F32)).astype(BF16)
    return hi, mid, lo


def _resident(shape):
    nd = len(shape)
    return pl.BlockSpec(shape, lambda *_: (0,) * nd, pipeline_mode=pl.Buffered(1))


def _layer(shape, layer):
    nd = len(shape)
    return pl.BlockSpec((None,) + tuple(shape), lambda *_: (layer,) + (0,) * nd,
                        pipeline_mode=pl.Buffered(1))


def _params(*sem):
    return pltpu.CompilerParams(dimension_semantics=sem, vmem_limit_bytes=VMEM_LIMIT_BYTES)


def _aug_base(head):
    return BF16_ROWS * (head // 2) + AUG_SPAN * (head % 2)


def _aug_selectors():
    sel_k = np.zeros((LANES, LANES), np.float32)
    sel_q = np.zeros((LANES, LANES), np.float32)
    for h in range(FOX_HEADS):
        base = _aug_base(h)
        for part in range(AUG_PARTS):
            sel_q[base + part, BF16_ROWS * part + h] = 1.0
            sel_q[base + AUG_PARTS + part, ONES_SLOT] = 1.0
            sel_k[AUG_PARTS * FOX_HEADS, base + part] = 1.0
            sel_k[FOX_HEADS * part + h, base + AUG_PARTS + part] = -1.0
    return jnp.asarray(sel_k, BF16), jnp.asarray(sel_q, BF16)


def _ffn_kernel(x_ref, g_ref, win_ref, wout_ref, gfin_ref, o_ref, acc_ref, *, final_norm):
    x = x_ref[...]
    xn = _rms(x, g_ref[...]).astype(BF16)
    for c in range(D_FF // FFN_CH):
        lo = c * FFN_CH
        a = _dot(xn, win_ref[:, lo:lo + FFN_CH])
        b = _dot(xn, win_ref[:, D_FF + lo:D_FF + lo + FFN_CH])
        h = (a * jax.nn.sigmoid(a) * b).astype(BF16)
        y = _dot(h, wout_ref[lo:lo + FFN_CH, :])
        if c == 0:
            acc_ref[...] = y
        else:
            acc_ref[...] += y
    out = x + 0.5 * acc_ref[...]
    if final_norm:
        out = _rms(out, gfin_ref[...])
    o_ref[...] = out


def _ffn(x, g, w_in, w_out, g_final, layer, final_norm):
    t = x.shape[0]
    tile = pl.BlockSpec((FFN_TM, D_MODEL), lambda i: (i, 0))
    return pl.pallas_call(
        functools.partial(_ffn_kernel, final_norm=final_norm),
        grid=(t // FFN_TM,),
        in_specs=[tile, _layer((1, D_MODEL), layer), _layer((D_MODEL, 2 * D_FF), layer),
                  _layer((D_FF, D_MODEL), layer), _resident((1, D_MODEL))],
        out_specs=tile,
        out_shape=jax.ShapeDtypeStruct((t, D_MODEL), F32),
        scratch_shapes=[pltpu.VMEM((FFN_TM, D_MODEL), F32)],
        compiler_params=_params("arbitrary"),
        name="ffn",
    )(x, g, w_in, w_out, g_final)


def _proj_kernel(x_ref, g_ref, w_ref, wqv_ref, wf_ref, wft_ref, bcol_ref, brow_ref,
                 selk_ref, selq_ref, cos_ref, sin_ref,
                 rq_ref, rk_ref, rv_ref, rg_ref, fk_ref, gr_ref, gf_ref,
                 fqt_ref, fvt_ref, kaug_ref, qaug_ref, carry_q, carry_k):
    tm = PROJ_TM

    @pl.when(pl.program_id(1) == 0)
    def _():
        carry_q[...] = jnp.zeros_like(carry_q)
        carry_k[...] = jnp.zeros_like(carry_k)

    h = _rms(x_ref[...], g_ref[...]).astype(BF16)

    def cols(off, width):
        return _dot(h, w_ref[:, off:off + width])

    cos = cos_ref[...]
    sin = sin_ref[...]

    def rope(t):
        parts = []
        for hd in range(RET_HEADS):
            th = t[:, hd * RET_DK:(hd + 1) * RET_DK]
            parts.append(th * cos + pltpu.roll(th, RET_DK // 2, 1) * sin)
        return jnp.concatenate(parts, axis=-1)

    rq_ref[...] = rope(cols(0, RET_W)).astype(BF16)
    rk_ref[...] = (rope(cols(RET_W, RET_W)) * (RET_DK ** -0.5)).astype(BF16)
    rv_ref[...] = cols(2 * RET_W, RET_W).astype(BF16)
    rg_ref[...] = cols(3 * RET_W, RET_W)
    fk_ref[...] = cols(4 * RET_W, FOX_W).astype(BF16)
    gr_ref[...] = jax.nn.sigmoid(cols(GATE_OFF, D_MODEL))
    gf_ref[...] = jax.nn.sigmoid(cols(GATE_OFF + D_MODEL, D_MODEL))

    fqt_ref[0] = (_dot_nt(wqv_ref[0:FOX_W, :], h) * (FOX_DH ** -0.5 * LOG2E)).astype(BF16)
    fvt_ref[0] = _dot_nt(wqv_ref[FOX_W:2 * FOX_W, :], h).astype(BF16)

    row = lax.broadcasted_iota(jnp.int32, (tm, tm), 0)
    col = lax.broadcasted_iota(jnp.int32, (tm, tm), 1)

    logf = jax.nn.log_sigmoid(_dot(h, wf_ref[...]) + bcol_ref[...])
    lower = jnp.where(col <= row, 1.0, 0.0).astype(BF16)
    c_k = carry_k[...] + sum(_dot(lower, part) for part in _split3(logf))
    carry_k[...] = c_k[tm - 1:tm, :]
    hi, mid, lo = _split3(c_k * LOG2E)
    lane = lax.broadcasted_iota(jnp.int32, (tm, LANES), 1)
    packed_k = jnp.where(lane < FOX_HEADS, hi, jnp.where(lane < 2 * FOX_HEADS, mid, lo))
    packed_k = jnp.where(lane == AUG_PARTS * FOX_HEADS, jnp.ones_like(packed_k), packed_k)
    kaug_ref[...] = _dot(packed_k, selk_ref[...]).astype(BF16)

    logft = jax.nn.log_sigmoid(_dot_nt(wft_ref[...], h) + brow_ref[...])
    upper = jnp.where(row <= col, 1.0, 0.0).astype(BF16)
    c_q = carry_q[:, 0:1] + sum(_dot(part, upper) for part in _split3(logft))
    carry_q[...] = jnp.broadcast_to(c_q[:, tm - 1:tm], carry_q.shape)
    hi, mid, lo = _split3(c_q * LOG2E)
    srow = lax.broadcasted_iota(jnp.int32, (BF16_ROWS, tm), 0)
    hi = jnp.where(srow == ONES_SLOT, jnp.ones_like(hi), hi)
    pad = jnp.zeros((LANES - AUG_PARTS * BF16_ROWS, tm), BF16)
    packed_q = jnp.concatenate([hi, mid, lo, pad], axis=0)
    qaug_ref[0] = _dot(selq_ref[...], packed_q).astype(BF16)


def _proj(x, g, w_main, wqv_t, wf, wft, bcol, brow, sel_k, sel_q, cos, sin, layer, batch, seq):
    t = x.shape[0]
    tm = PROJ_TM
    ns = seq // tm

    def tok(width):
        return pl.BlockSpec((tm, width), lambda b, s: (b * ns + s, 0))

    def feat(rows):
        return pl.BlockSpec((1, rows, tm), lambda b, s: (b, 0, s))

    pos = pl.BlockSpec((tm, RET_DK), lambda b, s: (s, 0))
    out_shapes = [jax.ShapeDtypeStruct((t, RET_W), BF16)] * 3 + [
        jax.ShapeDtypeStruct((t, RET_W), F32),
        jax.ShapeDtypeStruct((t, FOX_W), BF16)] + [
        jax.ShapeDtypeStruct((t, D_MODEL), F32)] * 2 + [
        jax.ShapeDtypeStruct((batch, FOX_W, seq), BF16)] * 2 + [
        jax.ShapeDtypeStruct((t, LANES), BF16),
        jax.ShapeDtypeStruct((batch, LANES, seq), BF16)]
    out_specs = [tok(RET_W)] * 4 + [tok(FOX_W)] + [tok(D_MODEL)] * 2 + [
        feat(FOX_W), feat(FOX_W), tok(LANES), feat(LANES)]
    return pl.pallas_call(
        _proj_kernel,
        grid=(batch, ns),
        in_specs=[tok(D_MODEL), _layer((1, D_MODEL), layer), _layer((D_MODEL, MAIN_COLS), layer),
                  _layer((2 * FOX_W, D_MODEL), layer),
                  _layer((D_MODEL, LANES), layer), _layer((BF16_ROWS, D_MODEL), layer),
                  _layer((1, LANES), layer), _layer((BF16_ROWS, 1), layer),
                  _resident((LANES, LANES)), _resident((LANES, LANES)), pos, pos],
        out_specs=out_specs,
        out_shape=out_shapes,
        scratch_shapes=[pltpu.VMEM((BF16_ROWS, LANES), F32), pltpu.VMEM((1, LANES), F32)],
        compiler_params=_params("arbitrary", "arbitrary"),
        name="mix_proj",
    )(x, g, w_main, wqv_t, wf, wft, bcol, brow, sel_k, sel_q, cos, sin)


def _ret_kernel(q_ref, k_ref, v_ref, rg_ref, rn_ref, o_ref, state_ref):
    c = RET_C

    @pl.when(pl.program_id(1) == 0)
    def _():
        state_ref[...] = jnp.zeros_like(state_ref)

    row = lax.broadcasted_iota(jnp.int32, (c, c), 0)
    col = lax.broadcasted_iota(jnp.int32, (c, c), 1)
    diff = (row - col).astype(F32)
    pos = lax.broadcasted_iota(jnp.int32, (c, 1), 0).astype(F32)
    for hd in range(RET_HEADS):
        log_gamma = float(np.log1p(-np.exp2(-5.0 - hd)))
        sl = slice(hd * RET_DK, (hd + 1) * RET_DK)
        q = q_ref[:, sl]
        k = k_ref[:, sl]
        v = v_ref[:, sl]
        decay = jnp.where(diff >= 0, jnp.exp(log_gamma * jnp.maximum(diff, 0.0)), 0.0)
        scores = _dot_nt(q, k) * decay
        inner = _dot(scores.astype(BF16), v)
        state = state_ref[hd]
        xi = jnp.exp(log_gamma * (pos + 1.0))
        cross = _dot(q, state.astype(BF16)) * xi
        zeta = jnp.exp(log_gamma * (c - 1.0 - pos))
        vz = (v.astype(F32) * zeta).astype(BF16)
        state_ref[hd] = float(np.exp(log_gamma * c)) * state + _dot_tn(k, vz)
        ret = inner + cross
        ret = ret * lax.rsqrt(jnp.mean(ret * ret, axis=-1, keepdims=True) + EPS)
        o_ref[:, sl] = (jax.nn.silu(rg_ref[:, sl]) * (ret * rn_ref[:, sl])).astype(BF16)


def _retention(rq, rk, rv, rg, ret_norm, layer, batch, seq):
    t = rq.shape[0]
    nc = seq // RET_C
    tile = pl.BlockSpec((RET_C, RET_W), lambda b, s: (b * nc + s, 0))
    return pl.pallas_call(
        _ret_kernel,
        grid=(batch, nc),
        in_specs=[tile, tile, tile, tile, _layer((1, RET_W), layer)],
        out_specs=tile,
        out_shape=jax.ShapeDtypeStruct((t, RET_W), BF16),
        scratch_shapes=[pltpu.VMEM((RET_HEADS, RET_DK, RET_DV), F32)],
        compiler_params=_params("arbitrary", "arbitrary"),
        name="retention",
    )(rq, rk, rv, rg, ret_norm)


def _fox_kernel(qt_ref, qaug_ref, k_ref, kaug_ref, vt_ref, o_ref,
                w_ref, s_ref, s2_ref, c_ref, c2_ref, p_ref, a_ref, m_ref, acc_ref):
    t = FOX_T
    first_head = pl.program_id(1) * FOX_GROUP
    qi = pl.program_id(2)
    rowi = lax.broadcasted_iota(jnp.int32, (LANES, t), 0)
    qaug = qaug_ref[0]
    for j in range(FOX_GROUP):
        qpair = qt_ref[0, (j // 2) * LANES:(j // 2 + 1) * LANES, :]
        zero = jnp.zeros_like(qpair)
        lo = (first_head + j) // 2 * BF16_ROWS + (j % 2) * AUG_SPAN
        top = jnp.where((rowi >= (j % 2) * FOX_DH) & (rowi < (j % 2 + 1) * FOX_DH), qpair, zero)
        bot = jnp.where((rowi >= lo) & (rowi < lo + AUG_SPAN), qaug, zero)
        w_ref[j, 0:LANES, :] = top
        w_ref[j, LANES:2 * LANES, :] = bot
    m_ref[...] = jnp.full_like(m_ref, NEG)
    acc_ref[...] = jnp.zeros_like(acc_ref)
    p_ref[...] = jnp.zeros_like(p_ref)
    a_ref[...] = jnp.ones_like(a_ref)

    def rows_of(kv):
        return pl.ds(pl.multiple_of(kv * t, t), t)

    def score_stage(kv, dst):
        dst_s, dst_c = dst
        rows = rows_of(kv)
        kaug = kaug_ref[rows, :]
        for j in range(FOX_GROUP):
            kblk = jnp.concatenate([k_ref[rows, (j // 2) * LANES:(j // 2 + 1) * LANES], kaug], axis=1)
            s = _dot(kblk, w_ref[j])
            dst_s[j] = s
            dst_c[j] = jnp.max(s, axis=0, keepdims=True)

    def softmax_stage(src, masked):
        src_s, src_c = src
        for j in range(FOX_GROUP):
            s = src_s[j]
            if masked:
                r = lax.broadcasted_iota(jnp.int32, (t, t), 0)
                cidx = lax.broadcasted_iota(jnp.int32, (t, t), 1)
                s = jnp.where(r <= cidx, s, NEG)
                cmax = jnp.max(s, axis=0, keepdims=True)
            else:
                cmax = src_c[j]
            m_old = m_ref[j]
            m_new = jnp.maximum(m_old, cmax)
            alpha = jnp.exp2(m_old - m_new)
            p = jnp.exp2(s - m_new)
            m_ref[j] = m_new
            a_ref[j] = alpha
            p_ref[j] = p.astype(BF16)

    def value_stage(kv):
        rows = rows_of(kv)
        ones = jnp.ones((BF16_ROWS, t), BF16)
        for j in range(FOX_GROUP):
            vt = jnp.concatenate([vt_ref[0, j * FOX_DH:(j + 1) * FOX_DH, rows], ones], axis=0)
            acc_ref[j] = a_ref[j] * acc_ref[j] + _dot(vt, p_ref[j])

    def step(i, src, dst):
        if dst is not src:
            score_stage(i + 1, dst)
        value_stage(jnp.maximum(i - 1, 0))
        softmax_stage(src, False)
        if dst is src:
            score_stage(i + 1, dst)

    buf_a = (s_ref, c_ref)
    buf_b = (s2_ref, c2_ref)

    def body(h, carry):
        step(2 * h, buf_a, buf_b)
        step(2 * h + 1, buf_b, buf_a)
        return carry

    score_stage(0, buf_a)
    lax.fori_loop(0, qi // 2, body, 0)

    @pl.when(qi % 2 == 1)
    def _():
        step(qi - 1, buf_a, buf_a)

    value_stage(jnp.maximum(qi - 1, 0))
    softmax_stage(buf_a, True)
    value_stage(qi)
    out_t = jnp.concatenate(
        [acc_ref[j, 0:FOX_DH, :] / acc_ref[j, FOX_DH:FOX_DH + 1, :] for j in range(FOX_GROUP)], axis=0)
    o_ref[...] = out_t.T.astype(BF16)


def _fox(fqt, qaug, fk, kaug, fvt, batch, seq):
    t = fk.shape[0]
    nq = seq // FOX_T
    gw = FOX_GROUP * FOX_DH
    return pl.pallas_call(
        _fox_kernel,
        grid=(batch, FOX_HEADS // FOX_GROUP, nq),
        in_specs=[
            pl.BlockSpec((1, gw, FOX_T), lambda b, g, i: (b, g, i)),
            pl.BlockSpec((1, LANES, FOX_T), lambda b, g, i: (b, 0, i)),
            pl.BlockSpec((seq, gw), lambda b, g, i: (b, g)),
            pl.BlockSpec((seq, LANES), lambda b, g, i: (b, 0)),
            pl.BlockSpec((1, gw, seq), lambda b, g, i: (b, g, 0)),
        ],
        out_specs=pl.BlockSpec((FOX_T, gw), lambda b, g, i: (b * nq + i, g)),
        out_shape=jax.ShapeDtypeStruct((t, FOX_W), BF16),
        scratch_shapes=[pltpu.VMEM((FOX_GROUP, 2 * LANES, FOX_T), BF16),
                        pltpu.VMEM((FOX_GROUP, FOX_T, FOX_T), F32),
                        pltpu.VMEM((FOX_GROUP, FOX_T, FOX_T), F32),
                        pltpu.VMEM((FOX_GROUP, 1, FOX_T), F32),
                        pltpu.VMEM((FOX_GROUP, 1, FOX_T), F32),
                        pltpu.VMEM((FOX_GROUP, FOX_T, FOX_T), BF16),
                        pltpu.VMEM((FOX_GROUP, 1, FOX_T), F32),
                        pltpu.VMEM((FOX_GROUP, 1, FOX_T), F32),
                        pltpu.VMEM((FOX_GROUP, FOX_DH + BF16_ROWS, FOX_T), F32)],
        compiler_params=_params("arbitrary", "arbitrary", "arbitrary"),
        name="fox_attn",
    )(fqt, qaug, fk, kaug, fvt)


def _mixout_kernel(x_ref, ret_ref, fox_ref, gr_ref, gf_ref, wor_ref, wof_ref, wout_ref, o_ref):
    y_ret = _dot(ret_ref[...], wor_ref[...])
    y_fox = _dot(fox_ref[...], wof_ref[...])
    merged = gr_ref[...] * y_ret + gf_ref[...] * y_fox
    o_ref[...] = x_ref[...] + _dot(merged.astype(BF16), wout_ref[...])


def _mixout(x, ret, fox, gr, gf, w_o_ret, w_o_fox, w_out, layer):
    t = x.shape[0]

    def tok(width):
        return pl.BlockSpec((OUT_TM, width), lambda i: (i, 0))

    return pl.pallas_call(
        _mixout_kernel,
        grid=(t // OUT_TM,),
        in_specs=[tok(D_MODEL), tok(RET_W), tok(FOX_W), tok(D_MODEL), tok(D_MODEL),
                  _layer((RET_W, D_MODEL), layer), _layer((FOX_W, D_MODEL), layer),
                  _layer((D_MODEL, D_MODEL), layer)],
        out_specs=tok(D_MODEL),
        out_shape=jax.ShapeDtypeStruct((t, D_MODEL), F32),
        compiler_params=_params("arbitrary"),
        name="mix_out",
    )(x, ret, fox, gr, gf, w_o_ret, w_o_fox, w_out)


def _rope_tables(seq):
    d = RET_DK
    inv = jnp.power(ROPE_BASE, -jnp.arange(0, d, 2, dtype=F32) / d)
    ang = jnp.arange(seq, dtype=F32)[:, None] * inv[None, :]
    cos, sin = jnp.cos(ang), jnp.sin(ang)
    return jnp.concatenate([cos, cos], axis=-1), jnp.concatenate([-sin, sin], axis=-1)


def kernel(x, norm_ffn1, w_ffn1_in, w_ffn1_out, norm_mix, w_in, b_forget, ret_norm,
           w_o_ret, w_o_fox, w_out, norm_ffn2, w_ffn2_in, w_ffn2_out, norm_final):
    batch, seq, d = x.shape
    assert d == D_MODEL and seq % PROJ_TM == 0 and (batch * seq) % FFN_TM == 0
    xt = x.reshape(batch * seq, d)

    w1i = w_ffn1_in.astype(BF16)
    w1o = w_ffn1_out.astype(BF16)
    w2i = w_ffn2_in.astype(BF16)
    w2o = w_ffn2_out.astype(BF16)
    fq_off = 4 * RET_W
    w_main = jnp.concatenate(
        [w_in[..., :fq_off], w_in[..., fq_off + FOX_W:fq_off + 2 * FOX_W],
         w_in[..., FF_OFF + FOX_HEADS:]], axis=-1).astype(BF16)
    wqv_t = jnp.swapaxes(jnp.concatenate(
        [w_in[..., fq_off:fq_off + FOX_W], w_in[..., fq_off + 2 * FOX_W:FF_OFF]], axis=-1), 1, 2).astype(BF16)
    w_f = w_in[..., FF_OFF:FF_OFF + FOX_HEADS]
    wf = jnp.pad(jnp.tile(w_f, (1, 1, AUG_PARTS)),
                 ((0, 0), (0, 0), (0, LANES - AUG_PARTS * FOX_HEADS))).astype(BF16)
    wft = jnp.pad(jnp.swapaxes(w_f, 1, 2), ((0, 0), (0, BF16_ROWS - FOX_HEADS), (0, 0))).astype(BF16)
    bcol = jnp.pad(jnp.tile(b_forget, (1, AUG_PARTS)),
                   ((0, 0), (0, LANES - AUG_PARTS * FOX_HEADS)))[:, None, :]
    brow = jnp.pad(b_forget, ((0, 0), (0, BF16_ROWS - FOX_HEADS)))[:, :, None]
    wor = w_o_ret.astype(BF16)
    wof = w_o_fox.astype(BF16)
    wo = w_out.astype(BF16)
    cos, sin = _rope_tables(seq)
    sel_k, sel_q = _aug_selectors()
    g_final = norm_final[None, :]
    g_ffn1 = norm_ffn1[:, None, :]
    g_mix = norm_mix[:, None, :]
    g_ffn2 = norm_ffn2[:, None, :]
    g_ret = ret_norm[:, None, :]

    for l in range(DEPTH):
        xt = _ffn(xt, g_ffn1, w1i, w1o, g_final, l, False)
        rq, rk, rv, rg, fk, gr, gf, fqt, fvt, kaug, qaug = _proj(
            xt, g_mix, w_main, wqv_t, wf, wft, bcol, brow, sel_k, sel_q, cos, sin, l, batch, seq)
        ret = _retention(rq, rk, rv, rg, g_ret, l, batch, seq)
        fox = _fox(fqt, qaug, fk, kaug, fvt, batch, seq)
        xt = _mixout(xt, ret, fox, gr, gf, wor, wof, wo, l)
        xt = _ffn(xt, g_ffn2, w2i, w2o, g_final, l, l == DEPTH - 1)
    return xt.reshape(batch, seq, d)
```

```python
import functools
import math

import jax
import jax.numpy as jnp
import numpy as np
from jax import lax
from jax.experimental import pallas as pl
from jax.experimental.pallas import tpu as pltpu

D_MODEL = 1024
DEPTH = 4
RET_HEADS = 4
RET_DK = 128
RET_DV = 128
FOX_HEADS = 8
FOX_DH = 64
D_FF = 2816
ROPE_BASE = 10000.0
EPS = 1e-6

RET_W = RET_HEADS * RET_DK
FOX_W = FOX_HEADS * FOX_DH
FF_OFF = 4 * RET_W + 3 * FOX_W
MAIN_COLS = 5 * RET_W + 2 * D_MODEL
GATE_OFF = 5 * RET_W

LANES = 128
MXU_COLS = 256
BF16_ROWS = 16
VMEM_LIMIT_BYTES = 56 * 1024 * 1024

FFN_TM = 1024
FFN_SUB = 512
FFN_CH = MXU_COLS
PROJ_TM = 512
RET_C = 256
RET_TT = 512
FOX_T = 256
FOX_GROUP = 8
OUT_TM = 1024
OUT_SUB = 256
NEG = -1e30
LOG2E = math.log2(math.e)

AUG_PARTS = 3
AUG_SPAN = 2 * AUG_PARTS

BF16 = jnp.bfloat16
F32 = jnp.float32


def _dot(a, b):
    return jnp.dot(a, b, preferred_element_type=F32)


def _dot_nt(a, b):
    return lax.dot_general(a, b, (((1,), (1,)), ((), ())), preferred_element_type=F32)


def _dot_tn(a, b):
    return lax.dot_general(a, b, (((0,), (0,)), ((), ())), preferred_element_type=F32)


def _rms(x, g):
    ms = jnp.mean(x * x, axis=-1, keepdims=True)
    return x * lax.rsqrt(ms + EPS) * g


def _split3(x):
    hi = x.astype(BF16)
    r = x - hi.astype(F32)
    mid = r.astype(BF16)
    lo = (r - mid.astype(F32)).astype(BF16)
    return hi, mid, lo


def _resident(shape):
    nd = len(shape)
    return pl.BlockSpec(shape, lambda *_: (0,) * nd, pipeline_mode=pl.Buffered(1))


def _layer(shape, layer):
    nd = len(shape)
    return pl.BlockSpec((None,) + tuple(shape), lambda *_: (layer,) + (0,) * nd,
                        pipeline_mode=pl.Buffered(1))


def _params(*sem):
    return pltpu.CompilerParams(dimension_semantics=sem, vmem_limit_bytes=VMEM_LIMIT_BYTES)


def _ffn_kernel(x_ref, g_ref, win_ref, wout_ref, gfin_ref, o_ref, acc_ref, *, final_norm):
    for sub in range(FFN_TM // FFN_SUB):
        rows = slice(sub * FFN_SUB, (sub + 1) * FFN_SUB)
        x = x_ref[rows, :]
        xn = _rms(x, g_ref[...]).astype(BF16)
        for c in range(D_FF // FFN_CH):
            lo = c * FFN_CH
            a = _dot(xn, win_ref[:, lo:lo + FFN_CH])
            b = _dot(xn, win_ref[:, D_FF + lo:D_FF + lo + FFN_CH])
            h = (a * jax.nn.sigmoid(a) * b).astype(BF16)
            y = _dot(h, wout_ref[lo:lo + FFN_CH, :])
            if c == 0:
                acc_ref[rows, :] = y
            else:
                acc_ref[rows, :] += y
        out = x + 0.5 * acc_ref[rows, :]
        if final_norm:
            out = _rms(out, gfin_ref[...])
        o_ref[rows, :] = out


def _ffn(x, g, w_in, w_out, g_final, layer, final_norm):
    t = x.shape[0]
    tile = pl.BlockSpec((FFN_TM, D_MODEL), lambda i: (i, 0))
    return pl.pallas_call(
        functools.partial(_ffn_kernel, final_norm=final_norm),
        grid=(t // FFN_TM,),
        in_specs=[tile, _layer((1, D_MODEL), layer), _layer((D_MODEL, 2 * D_FF), layer),
                  _layer((D_FF, D_MODEL), layer), _resident((1, D_MODEL))],
        out_specs=tile,
        out_shape=jax.ShapeDtypeStruct((t, D_MODEL), F32),
        scratch_shapes=[pltpu.VMEM((FFN_TM, D_MODEL), F32)],
        compiler_params=_params("arbitrary"),
        name="ffn",
    )(x, g, w_in, w_out, g_final)


def _proj_kernel(x_ref, g_ref, w_ref, wqvf_ref, brow_ref, cos_ref, sin_ref,
                 rq_ref, rk_ref, rv_ref, rg_ref, fk_ref, gr_ref, gf_ref,
                 fqt_ref, fvt_ref, kaug_ref, qaug_ref, carry):
    tm = PROJ_TM

    @pl.when(pl.program_id(1) == 0)
    def _():
        carry[...] = jnp.zeros_like(carry)

    h = _rms(x_ref[...], g_ref[...]).astype(BF16)

    qvf = _dot_nt(wqvf_ref[...], h)
    fqt_ref[0] = (qvf[0:FOX_W] * (FOX_DH ** -0.5 * LOG2E)).astype(BF16)
    fvt_ref[0] = qvf[FOX_W:2 * FOX_W].astype(BF16)

    c = jax.nn.log_sigmoid(qvf[2 * FOX_W:2 * FOX_W + BF16_ROWS] + brow_ref[...])
    lane = lax.broadcasted_iota(jnp.int32, (BF16_ROWS, tm), 1)
    shift = 1
    while shift < tm:
        c = c + jnp.where(lane >= shift, pltpu.roll(c, shift, 1), 0.0)
        shift *= 2
    c = c + carry[:, 0:1]
    carry[...] = jnp.broadcast_to(c[:, tm - 1:tm], carry.shape)

    parts = [part.astype(F32) for part in _split3(c * LOG2E)]
    srow = lax.broadcasted_iota(jnp.int32, (BF16_ROWS, tm), 0)
    q_blocks, k_blocks = [], []
    for pair in range(FOX_HEADS // 2):
        qb = jnp.zeros((BF16_ROWS, tm), F32)
        kb = jnp.zeros((BF16_ROWS, tm), F32)
        for j in range(2):
            base = j * AUG_SPAN
            for idx, part in enumerate(parts):
                src = jnp.broadcast_to(part[2 * pair + j:2 * pair + j + 1, :], (BF16_ROWS, tm))
                qb = jnp.where(srow == base + idx, src, qb)
                kb = jnp.where(srow == base + AUG_PARTS + idx, -src, kb)
            qb = jnp.where((srow >= base + AUG_PARTS) & (srow < base + AUG_SPAN), 1.0, qb)
            kb = jnp.where((srow >= base) & (srow < base + AUG_PARTS), 1.0, kb)
        q_blocks.append(qb)
        k_blocks.append(kb)
    pad = jnp.zeros((LANES - (FOX_HEADS // 2) * BF16_ROWS, tm), F32)
    qaug_ref[0] = jnp.concatenate(q_blocks + [pad], axis=0).astype(BF16)
    kaug_ref[...] = jnp.concatenate(k_blocks + [pad], axis=0).T.astype(BF16)

    def cols(off, width):
        return _dot(h, w_ref[:, off:off + width])

    cos = cos_ref[...]
    sin = sin_ref[...]

    def rope(t):
        parts = []
        for hd in range(RET_HEADS):
            th = t[:, hd * RET_DK:(hd + 1) * RET_DK]
            parts.append(th * cos + pltpu.roll(th, RET_DK // 2, 1) * sin)
        return jnp.concatenate(parts, axis=-1)

    rq_ref[...] = rope(cols(0, RET_W)).astype(BF16)
    rk_ref[...] = (rope(cols(RET_W, RET_W)) * (RET_DK ** -0.5)).astype(BF16)
    rv_ref[...] = cols(2 * RET_W, RET_W).astype(BF16)
    rg_ref[...] = cols(3 * RET_W, RET_W)
    fk_ref[...] = cols(4 * RET_W, FOX_W).astype(BF16)
    gr_ref[...] = jax.nn.sigmoid(cols(GATE_OFF, D_MODEL))
    gf_ref[...] = jax.nn.sigmoid(cols(GATE_OFF + D_MODEL, D_MODEL))


def _proj(x, g, w_main, wqvf_t, brow, cos, sin, layer, batch, seq):
    t = x.shape[0]
    tm = PROJ_TM
    ns = seq // tm

    def tok(width):
        return pl.BlockSpec((tm, width), lambda b, s: (b * ns + s, 0))

    def feat(rows):
        return pl.BlockSpec((1, rows, tm), lambda b, s: (b, 0, s))

    pos = pl.BlockSpec((tm, RET_DK), lambda b, s: (s, 0))
    out_shapes = [jax.ShapeDtypeStruct((t, RET_W), BF16)] * 3 + [
        jax.ShapeDtypeStruct((t, RET_W), F32),
        jax.ShapeDtypeStruct((t, FOX_W), BF16)] + [
        jax.ShapeDtypeStruct((t, D_MODEL), F32)] * 2 + [
        jax.ShapeDtypeStruct((batch, FOX_W, seq), BF16)] * 2 + [
        jax.ShapeDtypeStruct((t, LANES), BF16),
        jax.ShapeDtypeStruct((batch, LANES, seq), BF16)]
    out_specs = [tok(RET_W)] * 4 + [tok(FOX_W)] + [tok(D_MODEL)] * 2 + [
        feat(FOX_W), feat(FOX_W), tok(LANES), feat(LANES)]
    return pl.pallas_call(
        _proj_kernel,
        grid=(batch, ns),
        in_specs=[tok(D_MODEL), _layer((1, D_MODEL), layer), _layer((D_MODEL, MAIN_COLS), layer),
                  _layer((2 * FOX_W + BF16_ROWS, D_MODEL), layer),
                  _layer((BF16_ROWS, 1), layer), pos, pos],
        out_specs=out_specs,
        out_shape=out_shapes,
        scratch_shapes=[pltpu.VMEM((BF16_ROWS, LANES), F32)],
        compiler_params=_params("arbitrary", "arbitrary"),
        name="mix_proj",
    )(x, g, w_main, wqvf_t, brow, cos, sin)


def _ret_kernel(q_ref, k_ref, v_ref, rg_ref, rn_ref, o_ref, state_ref, decay_ref, xi_ref, zeta_ref):
    c = RET_C

    @pl.when((pl.program_id(0) == 0) & (pl.program_id(1) == 0))
    def _():
        row = lax.broadcasted_iota(jnp.int32, (c, c), 0)
        col = lax.broadcasted_iota(jnp.int32, (c, c), 1)
        diff = (row - col).astype(F32)
        pos = lax.broadcasted_iota(jnp.int32, (c, RET_DV), 0).astype(F32)
        for hd in range(RET_HEADS):
            log_gamma = float(np.log1p(-np.exp2(-5.0 - hd)))
            decay_ref[hd] = jnp.where(diff >= 0, jnp.exp(log_gamma * jnp.maximum(diff, 0.0)), 0.0)
            xi_ref[hd] = jnp.exp(log_gamma * (pos + 1.0))
            zeta_ref[hd] = jnp.exp(log_gamma * (c - 1.0 - pos))

    @pl.when(pl.program_id(1) == 0)
    def _():
        state_ref[...] = jnp.zeros_like(state_ref)

    for ch in range(RET_TT // c):
        rows = slice(ch * c, (ch + 1) * c)
        for hd in range(RET_HEADS):
            log_gamma = float(np.log1p(-np.exp2(-5.0 - hd)))
            sl = slice(hd * RET_DK, (hd + 1) * RET_DK)
            q = q_ref[rows, sl]
            k = k_ref[rows, sl]
            v = v_ref[rows, sl]
            scores = _dot_nt(q, k) * decay_ref[hd]
            inner = _dot(scores.astype(BF16), v)
            state = state_ref[hd]
            cross = _dot(q, state.astype(BF16)) * xi_ref[hd]
            vz = (v.astype(F32) * zeta_ref[hd]).astype(BF16)
            state_ref[hd] = float(np.exp(log_gamma * c)) * state + _dot_tn(k, vz)
            ret = inner + cross
            ret = ret * lax.rsqrt(jnp.mean(ret * ret, axis=-1, keepdims=True) + EPS)
            o_ref[rows, sl] = (jax.nn.silu(rg_ref[rows, sl]) * (ret * rn_ref[:, sl])).astype(BF16)


def _retention(rq, rk, rv, rg, ret_norm, layer, batch, seq):
    t = rq.shape[0]
    ns = seq // RET_TT
    tile = pl.BlockSpec((RET_TT, RET_W), lambda b, s: (b * ns + s, 0))
    return pl.pallas_call(
        _ret_kernel,
        grid=(batch, ns),
        in_specs=[tile, tile, tile, tile, _layer((1, RET_W), layer)],
        out_specs=tile,
        out_shape=jax.ShapeDtypeStruct((t, RET_W), BF16),
        scratch_shapes=[pltpu.VMEM((RET_HEADS, RET_DK, RET_DV), F32),
                        pltpu.VMEM((RET_HEADS, RET_C, RET_C), F32),
                        pltpu.VMEM((RET_HEADS, RET_C, RET_DV), F32),
                        pltpu.VMEM((RET_HEADS, RET_C, RET_DV), F32)],
        compiler_params=_params("arbitrary", "arbitrary"),
        name="retention",
    )(rq, rk, rv, rg, ret_norm)


def _fox_kernel(qt_ref, qaug_ref, k_ref, kaug_ref, vt_ref, o_ref,
                w_ref, s_ref, s2_ref, c_ref, c2_ref, p_ref, a_ref, m_ref, acc_ref):
    t = FOX_T
    first_head = pl.program_id(1) * FOX_GROUP
    qi = pl.program_id(2)
    rowi = lax.broadcasted_iota(jnp.int32, (LANES, t), 0)
    qaug = qaug_ref[0]
    for j in range(FOX_GROUP):
        qpair = qt_ref[0, (j // 2) * LANES:(j // 2 + 1) * LANES, :]
        zero = jnp.zeros_like(qpair)
        lo = (first_head + j) // 2 * BF16_ROWS + (j % 2) * AUG_SPAN
        top = jnp.where((rowi >= (j % 2) * FOX_DH) & (rowi < (j % 2 + 1) * FOX_DH), qpair, zero)
        bot = jnp.where((rowi >= lo) & (rowi < lo + AUG_SPAN), qaug, zero)
        w_ref[j, 0:LANES, :] = top
        w_ref[j, LANES:2 * LANES, :] = bot
    m_ref[...] = jnp.full_like(m_ref, NEG)
    acc_ref[...] = jnp.zeros_like(acc_ref)
    p_ref[...] = jnp.zeros_like(p_ref)
    a_ref[...] = jnp.ones_like(a_ref)

    def rows_of(kv):
        return pl.ds(pl.multiple_of(kv * t, t), t)

    def score_stage(kv, dst):
        dst_s, dst_c = dst
        rows = rows_of(kv)
        kaug = kaug_ref[rows, :]
        for j in range(FOX_GROUP):
            kblk = jnp.concatenate([k_ref[rows, (j // 2) * LANES:(j // 2 + 1) * LANES], kaug], axis=1)
            s = _dot(kblk, w_ref[j])
            dst_s[j] = s
            dst_c[j] = jnp.max(s, axis=0, keepdims=True)

    def softmax_stage(src, masked):
        src_s, src_c = src
        for j in range(FOX_GROUP):
            s = src_s[j]
            if masked:
                r = lax.broadcasted_iota(jnp.int32, (t, t), 0)
                cidx = lax.broadcasted_iota(jnp.int32, (t, t), 1)
                s = jnp.where(r <= cidx, s, NEG)
                cmax = jnp.max(s, axis=0, keepdims=True)
            else:
                cmax = src_c[j]
            m_old = m_ref[j]
            m_new = jnp.maximum(m_old, cmax)
            alpha = jnp.exp2(m_old - m_new)
            p = jnp.exp2(s - m_new)
            m_ref[j] = m_new
            a_ref[j] = alpha
            p_ref[j] = p.astype(BF16)

    def value_stage(kv):
        rows = rows_of(kv)
        ones = jnp.ones((BF16_ROWS, t), BF16)
        for j in range(FOX_GROUP):
            vt = jnp.concatenate([vt_ref[0, j * FOX_DH:(j + 1) * FOX_DH, rows], ones], axis=0)
            acc_ref[j] = a_ref[j] * acc_ref[j] + _dot(vt, p_ref[j])

    def step(i, src, dst):
        if dst is not src:
            score_stage(i + 1, dst)
        value_stage(jnp.maximum(i - 1, 0))
        softmax_stage(src, False)
        if dst is src:
            score_stage(i + 1, dst)

    buf_a = (s_ref, c_ref)
    buf_b = (s2_ref, c2_ref)

    def body(h, carry):
        step(2 * h, buf_a, buf_b)
        step(2 * h + 1, buf_b, buf_a)
        return carry

    score_stage(0, buf_a)
    lax.fori_loop(0, qi // 2, body, 0)

    @pl.when(qi % 2 == 1)
    def _():
        step(qi - 1, buf_a, buf_a)

    value_stage(jnp.maximum(qi - 1, 0))
    softmax_stage(buf_a, True)
    value_stage(qi)
    out_t = jnp.concatenate(
        [acc_ref[j, 0:FOX_DH, :] / acc_ref[j, FOX_DH:FOX_DH + 1, :] for j in range(FOX_GROUP)], axis=0)
    o_ref[...] = out_t.T.astype(BF16)


def _fox(fqt, qaug, fk, kaug, fvt, batch, seq):
    t = fk.shape[0]
    nq = seq // FOX_T
    gw = FOX_GROUP * FOX_DH
    return pl.pallas_call(
        _fox_kernel,
        grid=(batch, FOX_HEADS // FOX_GROUP, nq),
        in_specs=[
            pl.BlockSpec((1, gw, FOX_T), lambda b, g, i: (b, g, i)),
            pl.BlockSpec((1, LANES, FOX_T), lambda b, g, i: (b, 0, i)),
            pl.BlockSpec((seq, gw), lambda b, g, i: (b, g)),
            pl.BlockSpec((seq, LANES), lambda b, g, i: (b, 0)),
            pl.BlockSpec((1, gw, seq), lambda b, g, i: (b, g, 0)),
        ],
        out_specs=pl.BlockSpec((FOX_T, gw), lambda b, g, i: (b * nq + i, g)),
        out_shape=jax.ShapeDtypeStruct((t, FOX_W), BF16),
        scratch_shapes=[pltpu.VMEM((FOX_GROUP, 2 * LANES, FOX_T), BF16),
                        pltpu.VMEM((FOX_GROUP, FOX_T, FOX_T), F32),
                        pltpu.VMEM((FOX_GROUP, FOX_T, FOX_T), F32),
                        pltpu.VMEM((FOX_GROUP, 1, FOX_T), F32),
                        pltpu.VMEM((FOX_GROUP, 1, FOX_T), F32),
                        pltpu.VMEM((FOX_GROUP, FOX_T, FOX_T), BF16),
                        pltpu.VMEM((FOX_GROUP, 1, FOX_T), F32),
                        pltpu.VMEM((FOX_GROUP, 1, FOX_T), F32),
                        pltpu.VMEM((FOX_GROUP, FOX_DH + BF16_ROWS, FOX_T), F32)],
        compiler_params=_params("arbitrary", "arbitrary", "arbitrary"),
        name="fox_attn",
    )(fqt, qaug, fk, kaug, fvt)


def _mixout_kernel(x_ref, ret_ref, fox_ref, gr_ref, gf_ref, wor_ref, wof_ref, wout_ref, o_ref):
    for sub in range(OUT_TM // OUT_SUB):
        rows = slice(sub * OUT_SUB, (sub + 1) * OUT_SUB)
        y_ret = _dot(ret_ref[rows, :], wor_ref[...])
        y_fox = _dot(fox_ref[rows, :], wof_ref[...])
        merged = gr_ref[rows, :] * y_ret + gf_ref[rows, :] * y_fox
        o_ref[rows, :] = x_ref[rows, :] + _dot(merged.astype(BF16), wout_ref[...])


def _mixout(x, ret, fox, gr, gf, w_o_ret, w_o_fox, w_out, layer):
    t = x.shape[0]

    def tok(width):
        return pl.BlockSpec((OUT_TM, width), lambda i: (i, 0))

    return pl.pallas_call(
        _mixout_kernel,
        grid=(t // OUT_TM,),
        in_specs=[tok(D_MODEL), tok(RET_W), tok(FOX_W), tok(D_MODEL), tok(D_MODEL),
                  _layer((RET_W, D_MODEL), layer), _layer((FOX_W, D_MODEL), layer),
                  _layer((D_MODEL, D_MODEL), layer)],
        out_specs=tok(D_MODEL),
        out_shape=jax.ShapeDtypeStruct((t, D_MODEL), F32),
        compiler_params=_params("arbitrary"),
        name="mix_out",
    )(x, ret, fox, gr, gf, w_o_ret, w_o_fox, w_out)


def _rope_tables(seq):
    d = RET_DK
    inv = jnp.power(ROPE_BASE, -jnp.arange(0, d, 2, dtype=F32) / d)
    ang = jnp.arange(seq, dtype=F32)[:, None] * inv[None, :]
    cos, sin = jnp.cos(ang), jnp.sin(ang)
    return jnp.concatenate([cos, cos], axis=-1), jnp.concatenate([-sin, sin], axis=-1)


def kernel(x, norm_ffn1, w_ffn1_in, w_ffn1_out, norm_mix, w_in, b_forget, ret_norm,
           w_o_ret, w_o_fox, w_out, norm_ffn2, w_ffn2_in, w_ffn2_out, norm_final):
    batch, seq, d = x.shape
    assert d == D_MODEL and seq % PROJ_TM == 0 and (batch * seq) % FFN_TM == 0
    xt = x.reshape(batch * seq, d)

    w1i = w_ffn1_in.astype(BF16)
    w1o = w_ffn1_out.astype(BF16)
    w2i = w_ffn2_in.astype(BF16)
    w2o = w_ffn2_out.astype(BF16)
    fq_off = 4 * RET_W
    w_main = jnp.concatenate(
        [w_in[..., :fq_off], w_in[..., fq_off + FOX_W:fq_off + 2 * FOX_W],
         w_in[..., FF_OFF + FOX_HEADS:]], axis=-1).astype(BF16)
    wqvf_t = jnp.swapaxes(jnp.concatenate(
        [w_in[..., fq_off:fq_off + FOX_W], w_in[..., fq_off + 2 * FOX_W:FF_OFF + FOX_HEADS],
         jnp.zeros((DEPTH, D_MODEL, BF16_ROWS - FOX_HEADS), w_in.dtype)], axis=-1), 1, 2).astype(BF16)
    brow = jnp.pad(b_forget, ((0, 0), (0, BF16_ROWS - FOX_HEADS)))[:, :, None]
    wor = w_o_ret.astype(BF16)
    wof = w_o_fox.astype(BF16)
    wo = w_out.astype(BF16)
    cos, sin = _rope_tables(seq)
    g_final = norm_final[None, :]
    g_ffn1 = norm_ffn1[:, None, :]
    g_mix = norm_mix[:, None, :]
    g_ffn2 = norm_ffn2[:, None, :]
    g_ret = ret_norm[:, None, :]

    for l in range(DEPTH):
        xt = _ffn(xt, g_ffn1, w1i, w1o, g_final, l, False)
        rq, rk, rv, rg, fk, gr, gf, fqt, fvt, kaug, qaug = _proj(
            xt, g_mix, w_main, wqvf_t, brow, cos, sin, l, batch, seq)
        ret = _retention(rq, rk, rv, rg, g_ret, l, batch, seq)
        fox = _fox(fqt, qaug, fk, kaug, fvt, batch, seq)
        xt = _mixout(xt, ret, fox, gr, gf, wor, wof, wo, l)
        xt = _ffn(xt, g_ffn2, w2i, w2o, g_final, l, l == DEPTH - 1)
    return xt.reshape(batch, seq, d)
```

```python
import functools
import math

import jax
import jax.numpy as jnp
import numpy as np
from jax import lax
from jax.experimental import pallas as pl
from jax.experimental.pallas import tpu as pltpu

D_MODEL = 1024
DEPTH = 4
RET_HEADS = 4
RET_DK = 128
RET_DV = 128
FOX_HEADS = 8
FOX_DH = 64
D_FF = 2816
ROPE_BASE = 10000.0
EPS = 1e-6

RET_W = RET_HEADS * RET_DK
FOX_W = FOX_HEADS * FOX_DH
FF_OFF = 4 * RET_W + 3 * FOX_W
MAIN_COLS = 5 * RET_W + 2 * D_MODEL
GATE_OFF = 5 * RET_W

LANES = 128
MXU_COLS = 256
BF16_ROWS = 16
VMEM_LIMIT_BYTES = 56 * 1024 * 1024

FFN_TM = 1024
FFN_SUB = 512
FFN_CH = MXU_COLS
PROJ_TM = 1024
PROJ_SUB = 512
RET_C = 256
RET_TT = 512
FOX_TQ = 512
FOX_TK = 256
FOX_GROUP = 8
OUT_TM = 1024
OUT_SUB = 256
NEG = -1e30
LOG2E = math.log2(math.e)

AUG_PARTS = 3
AUG_SPAN = 2 * AUG_PARTS

BF16 = jnp.bfloat16
F32 = jnp.float32


def _dot(a, b):
    return jnp.dot(a, b, preferred_element_type=F32)


def _dot_nt(a, b):
    return lax.dot_general(a, b, (((1,), (1,)), ((), ())), preferred_element_type=F32)


def _dot_tn(a, b):
    return lax.dot_general(a, b, (((0,), (0,)), ((), ())), preferred_element_type=F32)


def _rms(x, g):
    ms = jnp.mean(x * x, axis=-1, keepdims=True)
    return x * lax.rsqrt(ms + EPS) * g


def _split3(x):
    hi = x.astype(BF16)
    r = x - hi.astype(F32)
    mid = r.astype(BF16)
    lo = (r - mid.astype(F32)).astype(BF16)
    return hi, mid, lo


def _resident(shape):
    nd = len(shape)
    return pl.BlockSpec(shape, lambda *_: (0,) * nd, pipeline_mode=pl.Buffered(1))


def _layer(shape, layer):
    nd = len(shape)
    return pl.BlockSpec((None,) + tuple(shape), lambda *_: (layer,) + (0,) * nd,
                        pipeline_mode=pl.Buffered(1))


def _params(*sem):
    return pltpu.CompilerParams(dimension_semantics=sem, vmem_limit_bytes=VMEM_LIMIT_BYTES)


def _ffn_kernel(x_ref, g_ref, win_ref, wout_ref, gfin_ref, o_ref, acc_ref, *, final_norm):
    for sub in range(FFN_TM // FFN_SUB):
        rows = slice(sub * FFN_SUB, (sub + 1) * FFN_SUB)
        x = x_ref[rows, :]
        xn = _rms(x, g_ref[...]).astype(BF16)
        for c in range(D_FF // FFN_CH):
            lo = c * FFN_CH
            a = _dot(xn, win_ref[:, lo:lo + FFN_CH])
            b = _dot(xn, win_ref[:, D_FF + lo:D_FF + lo + FFN_CH])
            h = (a * jax.nn.sigmoid(a) * b).astype(BF16)
            y = _dot(h, wout_ref[lo:lo + FFN_CH, :])
            if c == 0:
                acc_ref[rows, :] = y
            else:
                acc_ref[rows, :] += y
        out = x + 0.5 * acc_ref[rows, :]
        if final_norm:
            out = _rms(out, gfin_ref[...])
        o_ref[rows, :] = out


def _ffn(x, g, w_in, w_out, g_final, layer, final_norm):
    t = x.shape[0]
    tile = pl.BlockSpec((FFN_TM, D_MODEL), lambda i: (i, 0))
    return pl.pallas_call(
        functools.partial(_ffn_kernel, final_norm=final_norm),
        grid=(t // FFN_TM,),
        in_specs=[tile, _layer((1, D_MODEL), layer), _layer((D_MODEL, 2 * D_FF), layer),
                  _layer((D_FF, D_MODEL), layer), _resident((1, D_MODEL))],
        out_specs=tile,
        out_shape=jax.ShapeDtypeStruct((t, D_MODEL), F32),
        scratch_shapes=[pltpu.VMEM((FFN_TM, D_MODEL), F32)],
        compiler_params=_params("arbitrary"),
        name="ffn",
    )(x, g, w_in, w_out, g_final)


def _proj_kernel(x_ref, g_ref, w_ref, wqvf_ref, brow_ref, cos_ref, sin_ref,
                 rq_ref, rk_ref, rv_ref, rg_ref, fk_ref, gr_ref, gf_ref,
                 fqt_ref, fvt_ref, kaug_ref, qaug_ref, carry):
    @pl.when(pl.program_id(1) == 0)
    def _():
        carry[...] = jnp.zeros_like(carry)

    for sub in range(PROJ_TM // PROJ_SUB):
        _proj_rows(slice(sub * PROJ_SUB, (sub + 1) * PROJ_SUB),
                   x_ref, g_ref, w_ref, wqvf_ref, brow_ref, cos_ref, sin_ref,
                   rq_ref, rk_ref, rv_ref, rg_ref, fk_ref, gr_ref, gf_ref,
                   fqt_ref, fvt_ref, kaug_ref, qaug_ref, carry)


def _proj_rows(rows, x_ref, g_ref, w_ref, wqvf_ref, brow_ref, cos_ref, sin_ref,
               rq_ref, rk_ref, rv_ref, rg_ref, fk_ref, gr_ref, gf_ref,
               fqt_ref, fvt_ref, kaug_ref, qaug_ref, carry):
    tm = PROJ_SUB
    h = _rms(x_ref[rows, :], g_ref[...]).astype(BF16)

    qvf = _dot_nt(wqvf_ref[...], h)
    fqt_ref[0, :, rows] = (qvf[0:FOX_W] * (FOX_DH ** -0.5 * LOG2E)).astype(BF16)
    fvt_ref[0, :, rows] = qvf[FOX_W:2 * FOX_W].astype(BF16)

    c = jax.nn.log_sigmoid(qvf[2 * FOX_W:2 * FOX_W + BF16_ROWS] + brow_ref[...])
    lane = lax.broadcasted_iota(jnp.int32, (BF16_ROWS, tm), 1)
    shift = 1
    while shift < tm:
        c = c + jnp.where(lane >= shift, pltpu.roll(c, shift, 1), 0.0)
        shift *= 2
    c = c + carry[:, 0:1]
    carry[...] = jnp.broadcast_to(c[:, tm - 1:tm], carry.shape)

    parts = [part.astype(F32) for part in _split3(c * LOG2E)]
    srow = lax.broadcasted_iota(jnp.int32, (BF16_ROWS, tm), 0)
    q_blocks, k_blocks = [], []
    for pair in range(FOX_HEADS // 2):
        qb = jnp.zeros((BF16_ROWS, tm), F32)
        kb = jnp.zeros((BF16_ROWS, tm), F32)
        for j in range(2):
            base = j * AUG_SPAN
            for idx, part in enumerate(parts):
                src = jnp.broadcast_to(part[2 * pair + j:2 * pair + j + 1, :], (BF16_ROWS, tm))
                qb = jnp.where(srow == base + idx, src, qb)
                kb = jnp.where(srow == base + AUG_PARTS + idx, -src, kb)
            qb = jnp.where((srow >= base + AUG_PARTS) & (srow < base + AUG_SPAN), 1.0, qb)
            kb = jnp.where((srow >= base) & (srow < base + AUG_PARTS), 1.0, kb)
        q_blocks.append(qb)
        k_blocks.append(kb)
    pad = jnp.zeros((LANES - (FOX_HEADS // 2) * BF16_ROWS, tm), F32)
    qaug_ref[0, :, rows] = jnp.concatenate(q_blocks + [pad], axis=0).astype(BF16)
    kaug_ref[rows, :] = jnp.concatenate(k_blocks + [pad], axis=0).T.astype(BF16)

    def cols(off, width):
        return _dot(h, w_ref[:, off:off + width])

    cos = cos_ref[rows, :]
    sin = sin_ref[rows, :]

    def rope(t):
        parts = []
        for hd in range(RET_HEADS):
            th = t[:, hd * RET_DK:(hd + 1) * RET_DK]
            parts.append(th * cos + pltpu.roll(th, RET_DK // 2, 1) * sin)
        return jnp.concatenate(parts, axis=-1)

    rq_ref[rows, :] = rope(cols(0, RET_W)).astype(BF16)
    rk_ref[rows, :] = (rope(cols(RET_W, RET_W)) * (RET_DK ** -0.5)).astype(BF16)
    rv_ref[rows, :] = cols(2 * RET_W, RET_W).astype(BF16)
    rg_ref[rows, :] = cols(3 * RET_W, RET_W).astype(BF16)
    fk_ref[rows, :] = cols(4 * RET_W, FOX_W).astype(BF16)
    gr_ref[rows, :] = jax.nn.sigmoid(cols(GATE_OFF, D_MODEL)).astype(BF16)
    gf_ref[rows, :] = jax.nn.sigmoid(cols(GATE_OFF + D_MODEL, D_MODEL)).astype(BF16)


def _proj(x, g, w_main, wqvf_t, brow, cos, sin, layer, batch, seq):
    t = x.shape[0]
    tm = PROJ_TM
    ns = seq // tm

    def tok(width):
        return pl.BlockSpec((tm, width), lambda b, s: (b * ns + s, 0))

    def feat(rows):
        return pl.BlockSpec((1, rows, tm), lambda b, s: (b, 0, s))

    pos = pl.BlockSpec((tm, RET_DK), lambda b, s: (s, 0))
    out_shapes = [jax.ShapeDtypeStruct((t, RET_W), BF16)] * 4 + [
        jax.ShapeDtypeStruct((t, FOX_W), BF16)] + [
        jax.ShapeDtypeStruct((t, D_MODEL), BF16)] * 2 + [
        jax.ShapeDtypeStruct((batch, FOX_W, seq), BF16)] * 2 + [
        jax.ShapeDtypeStruct((t, LANES), BF16),
        jax.ShapeDtypeStruct((batch, LANES, seq), BF16)]
    out_specs = [tok(RET_W)] * 4 + [tok(FOX_W)] + [tok(D_MODEL)] * 2 + [
        feat(FOX_W), feat(FOX_W), tok(LANES), feat(LANES)]
    return pl.pallas_call(
        _proj_kernel,
        grid=(batch, ns),
        in_specs=[tok(D_MODEL), _layer((1, D_MODEL), layer), _layer((D_MODEL, MAIN_COLS), layer),
                  _layer((2 * FOX_W + BF16_ROWS, D_MODEL), layer),
                  _layer((BF16_ROWS, 1), layer), pos, pos],
        out_specs=out_specs,
        out_shape=out_shapes,
        scratch_shapes=[pltpu.VMEM((BF16_ROWS, LANES), F32)],
        compiler_params=_params("arbitrary", "arbitrary"),
        name="mix_proj",
    )(x, g, w_main, wqvf_t, brow, cos, sin)


def _ret_kernel(q_ref, k_ref, v_ref, rg_ref, rn_ref, o_ref, state_ref, decay_ref, xi_ref, zeta_ref):
    c = RET_C

    @pl.when((pl.program_id(0) == 0) & (pl.program_id(1) == 0))
    def _():
        row = lax.broadcasted_iota(jnp.int32, (c, c), 0)
        col = lax.broadcasted_iota(jnp.int32, (c, c), 1)
        diff = (row - col).astype(F32)
        pos = lax.broadcasted_iota(jnp.int32, (c, RET_DV), 0).astype(F32)
        for hd in range(RET_HEADS):
            log_gamma = float(np.log1p(-np.exp2(-5.0 - hd)))
            decay_ref[hd] = jnp.where(diff >= 0, jnp.exp(log_gamma * jnp.maximum(diff, 0.0)), 0.0)
            xi_ref[hd] = jnp.exp(log_gamma * (pos + 1.0))
            zeta_ref[hd] = jnp.exp(log_gamma * (c - 1.0 - pos))

    @pl.when(pl.program_id(1) == 0)
    def _():
        state_ref[...] = jnp.zeros_like(state_ref)

    for ch in range(RET_TT // c):
        rows = slice(ch * c, (ch + 1) * c)
        for hd in range(RET_HEADS):
            log_gamma = float(np.log1p(-np.exp2(-5.0 - hd)))
            sl = slice(hd * RET_DK, (hd + 1) * RET_DK)
            q = q_ref[rows, sl]
            k = k_ref[rows, sl]
            v = v_ref[rows, sl]
            scores = _dot_nt(q, k) * decay_ref[hd]
            inner = _dot(scores.astype(BF16), v)
            state = state_ref[hd]
            cross = _dot(q, state.astype(BF16)) * xi_ref[hd]
            vz = (v.astype(F32) * zeta_ref[hd]).astype(BF16)
            state_ref[hd] = float(np.exp(log_gamma * c)) * state + _dot_tn(k, vz)
            ret = inner + cross
            ret = ret * lax.rsqrt(jnp.mean(ret * ret, axis=-1, keepdims=True) + EPS)
            gate = jax.nn.silu(rg_ref[rows, sl].astype(F32))
            o_ref[rows, sl] = (gate * (ret * rn_ref[:, sl])).astype(BF16)


def _retention(rq, rk, rv, rg, ret_norm, layer, batch, seq):
    t = rq.shape[0]
    ns = seq // RET_TT
    tile = pl.BlockSpec((RET_TT, RET_W), lambda b, s: (b * ns + s, 0))
    return pl.pallas_call(
        _ret_kernel,
        grid=(batch, ns),
        in_specs=[tile, tile, tile, tile, _layer((1, RET_W), layer)],
        out_specs=tile,
        out_shape=jax.ShapeDtypeStruct((t, RET_W), BF16),
        scratch_shapes=[pltpu.VMEM((RET_HEADS, RET_DK, RET_DV), F32),
                        pltpu.VMEM((RET_HEADS, RET_C, RET_C), F32),
                        pltpu.VMEM((RET_HEADS, RET_C, RET_DV), F32),
                        pltpu.VMEM((RET_HEADS, RET_C, RET_DV), F32)],
        compiler_params=_params("arbitrary", "arbitrary"),
        name="retention",
    )(rq, rk, rv, rg, ret_norm)


def _fox_kernel(qt_ref, qaug_ref, k_ref, kaug_ref, vt_ref, o_ref,
                w_ref, s_ref, s2_ref, c_ref, c2_ref, p_ref, a_ref, m_ref, acc_ref):
    tq, tk = FOX_TQ, FOX_TK
    first_head = pl.program_id(1) * FOX_GROUP
    qi = pl.program_id(2)
    rowi = lax.broadcasted_iota(jnp.int32, (LANES, tq), 0)
    qaug = qaug_ref[0]
    for j in range(FOX_GROUP):
        qpair = qt_ref[0, (j // 2) * LANES:(j // 2 + 1) * LANES, :]
        zero = jnp.zeros_like(qpair)
        lo = (first_head + j) // 2 * BF16_ROWS + (j % 2) * AUG_SPAN
        top = jnp.where((rowi >= (j % 2) * FOX_DH) & (rowi < (j % 2 + 1) * FOX_DH), qpair, zero)
        bot = jnp.where((rowi >= lo) & (rowi < lo + AUG_SPAN), qaug, zero)
        w_ref[j, 0:LANES, :] = top
        w_ref[j, LANES:2 * LANES, :] = bot
    m_ref[...] = jnp.full_like(m_ref, NEG)
    acc_ref[...] = jnp.zeros_like(acc_ref)
    p_ref[...] = jnp.zeros_like(p_ref)
    a_ref[...] = jnp.ones_like(a_ref)

    def rows_of(kv):
        return pl.ds(pl.multiple_of(kv * tk, tk), tk)

    def score_head(j, kv, dst):
        dst_s, dst_c = dst
        rows = rows_of(kv)
        kblk = jnp.concatenate(
            [k_ref[rows, (j // 2) * LANES:(j // 2 + 1) * LANES], kaug_ref[rows, :]], axis=1)
        s = _dot(kblk, w_ref[j])
        dst_s[j] = s
        dst_c[j] = jnp.max(s, axis=0, keepdims=True)

    def softmax_head(j, src, diag):
        src_s, src_c = src
        s = src_s[j]
        if diag is not None:
            r = lax.broadcasted_iota(jnp.int32, (tk, tq), 0)
            cidx = lax.broadcasted_iota(jnp.int32, (tk, tq), 1)
            s = jnp.where(r + diag * tk <= cidx, s, NEG)
            cmax = jnp.max(s, axis=0, keepdims=True)
        else:
            cmax = src_c[j]
        m_old = m_ref[j]
        m_new = jnp.maximum(m_old, cmax)
        alpha = jnp.exp2(m_old - m_new)
        p = jnp.exp2(s - m_new)
        m_ref[j] = m_new
        a_ref[j] = alpha
        p_ref[j] = p.astype(BF16)

    def value_head(j, kv):
        rows = rows_of(kv)
        ones = jnp.ones((BF16_ROWS, tk), BF16)
        vt = jnp.concatenate([vt_ref[0, j * FOX_DH:(j + 1) * FOX_DH, rows], ones], axis=0)
        acc_ref[j] = a_ref[j] * acc_ref[j] + _dot(vt, p_ref[j])

    def score_stage(kv, dst):
        for j in range(FOX_GROUP):
            score_head(j, kv, dst)

    def softmax_stage(src, diag):
        for j in range(FOX_GROUP):
            softmax_head(j, src, diag)

    def value_stage(kv):
        for j in range(FOX_GROUP):
            value_head(j, kv)

    def step(i, src, dst, diag=None):
        prev = jnp.maximum(i - 1, 0)
        for j in range(FOX_GROUP):
            score_head(j, i + 1, dst)
            value_head(j, prev)
            softmax_head(j, src, diag)

    buf_a = (s_ref, c_ref)
    buf_b = (s2_ref, c2_ref)

    def body(h, carry):
        step(2 * h, buf_a, buf_b)
        step(2 * h + 1, buf_b, buf_a)
        return carry

    assert tq == 2 * tk
    first_diag = 2 * qi
    score_stage(0, buf_a)
    lax.fori_loop(0, qi, body, 0)
    step(first_diag, buf_a, buf_b, diag=0)

    @pl.when(qi >= 0)
    def _():
        value_stage(first_diag)
        softmax_stage(buf_b, 1)

    @pl.when(qi >= 0)
    def _():
        value_stage(first_diag + 1)
    out_t = jnp.concatenate(
        [acc_ref[j, 0:FOX_DH, :] / acc_ref[j, FOX_DH:FOX_DH + 1, :] for j in range(FOX_GROUP)], axis=0)
    o_ref[...] = out_t.T.astype(BF16)


def _fox(fqt, qaug, fk, kaug, fvt, batch, seq):
    t = fk.shape[0]
    tq, tk = FOX_TQ, FOX_TK
    nq = seq // tq
    gw = FOX_GROUP * FOX_DH
    return pl.pallas_call(
        _fox_kernel,
        grid=(batch, FOX_HEADS // FOX_GROUP, nq),
        in_specs=[
            pl.BlockSpec((1, gw, tq), lambda b, g, i: (b, g, i)),
            pl.BlockSpec((1, LANES, tq), lambda b, g, i: (b, 0, i)),
            pl.BlockSpec((seq, gw), lambda b, g, i: (b, g)),
            pl.BlockSpec((seq, LANES), lambda b, g, i: (b, 0)),
            pl.BlockSpec((1, gw, seq), lambda b, g, i: (b, g, 0)),
        ],
        out_specs=pl.BlockSpec((tq, gw), lambda b, g, i: (b * nq + i, g)),
        out_shape=jax.ShapeDtypeStruct((t, FOX_W), BF16),
        scratch_shapes=[pltpu.VMEM((FOX_GROUP, 2 * LANES, tq), BF16),
                        pltpu.VMEM((FOX_GROUP, tk, tq), F32),
                        pltpu.VMEM((FOX_GROUP, tk, tq), F32),
                        pltpu.VMEM((FOX_GROUP, 1, tq), F32),
                        pltpu.VMEM((FOX_GROUP, 1, tq), F32),
                        pltpu.VMEM((FOX_GROUP, tk, tq), BF16),
                        pltpu.VMEM((FOX_GROUP, 1, tq), F32),
                        pltpu.VMEM((FOX_GROUP, 1, tq), F32),
                        pltpu.VMEM((FOX_GROUP, FOX_DH + BF16_ROWS, tq), F32)],
        compiler_params=_params("arbitrary", "arbitrary", "arbitrary"),
        name="fox_attn",
    )(fqt, qaug, fk, kaug, fvt)


def _mixout_kernel(x_ref, ret_ref, fox_ref, gr_ref, gf_ref, wor_ref, wof_ref, wout_ref, o_ref):
    for sub in range(OUT_TM // OUT_SUB):
        rows = slice(sub * OUT_SUB, (sub + 1) * OUT_SUB)
        y_ret = _dot(ret_ref[rows, :], wor_ref[...])
        y_fox = _dot(fox_ref[rows, :], wof_ref[...])
        merged = gr_ref[rows, :].astype(F32) * y_ret + gf_ref[rows, :].astype(F32) * y_fox
        o_ref[rows, :] = x_ref[rows, :] + _dot(merged.astype(BF16), wout_ref[...])


def _mixout(x, ret, fox, gr, gf, w_o_ret, w_o_fox, w_out, layer):
    t = x.shape[0]

    def tok(width):
        return pl.BlockSpec((OUT_TM, width), lambda i: (i, 0))

    return pl.pallas_call(
        _mixout_kernel,
        grid=(t // OUT_TM,),
        in_specs=[tok(D_MODEL), tok(RET_W), tok(FOX_W), tok(D_MODEL), tok(D_MODEL),
                  _layer((RET_W, D_MODEL), layer), _layer((FOX_W, D_MODEL), layer),
                  _layer((D_MODEL, D_MODEL), layer)],
        out_specs=tok(D_MODEL),
        out_shape=jax.ShapeDtypeStruct((t, D_MODEL), F32),
        compiler_params=_params("arbitrary"),
        name="mix_out",
    )(x, ret, fox, gr, gf, w_o_ret, w_o_fox, w_out)


def _rope_tables(seq):
    d = RET_DK
    inv = jnp.power(ROPE_BASE, -jnp.arange(0, d, 2, dtype=F32) / d)
    ang = jnp.arange(seq, dtype=F32)[:, None] * inv[None, :]
    cos, sin = jnp.cos(ang), jnp.sin(ang)
    return jnp.concatenate([cos, cos], axis=-1), jnp.concatenate([-sin, sin], axis=-1)


def kernel(x, norm_ffn1, w_ffn1_in, w_ffn1_out, norm_mix, w_in, b_forget, ret_norm,
           w_o_ret, w_o_fox, w_out, norm_ffn2, w_ffn2_in, w_ffn2_out, norm_final):
    batch, seq, d = x.shape
    assert d == D_MODEL and all(seq % tile == 0 for tile in (PROJ_TM, RET_TT, FOX_TQ))
    assert all((batch * seq) % tile == 0 for tile in (FFN_TM, OUT_TM))
    xt = x.reshape(batch * seq, d)

    w1i = w_ffn1_in.astype(BF16)
    w1o = w_ffn1_out.astype(BF16)
    w2i = w_ffn2_in.astype(BF16)
    w2o = w_ffn2_out.astype(BF16)
    fq_off = 4 * RET_W
    w_main = jnp.concatenate(
        [w_in[..., :fq_off], w_in[..., fq_off + FOX_W:fq_off + 2 * FOX_W],
         w_in[..., FF_OFF + FOX_HEADS:]], axis=-1).astype(BF16)
    wqvf_t = jnp.swapaxes(jnp.concatenate(
        [w_in[..., fq_off:fq_off + FOX_W], w_in[..., fq_off + 2 * FOX_W:FF_OFF + FOX_HEADS],
         jnp.zeros((DEPTH, D_MODEL, BF16_ROWS - FOX_HEADS), w_in.dtype)], axis=-1), 1, 2).astype(BF16)
    brow = jnp.pad(b_forget, ((0, 0), (0, BF16_ROWS - FOX_HEADS)))[:, :, None]
    wor = w_o_ret.astype(BF16)
    wof = w_o_fox.astype(BF16)
    wo = w_out.astype(BF16)
    cos, sin = _rope_tables(seq)
    g_final = norm_final[None, :]
    g_ffn1 = norm_ffn1[:, None, :]
    g_mix = norm_mix[:, None, :]
    g_ffn2 = norm_ffn2[:, None, :]
    g_ret = ret_norm[:, None, :]

    for l in range(DEPTH):
        xt = _ffn(xt, g_ffn1, w1i, w1o, g_final, l, False)
        rq, rk, rv, rg, fk, gr, gf, fqt, fvt, kaug, qaug = _proj(
            xt, g_mix, w_main, wqvf_t, brow, cos, sin, l, batch, seq)
        ret = _retention(rq, rk, rv, rg, g_ret, l, batch, seq)
        fox = _fox(fqt, qaug, fk, kaug, fvt, batch, seq)
        xt = _mixout(xt, ret, fox, gr, gf, wor, wof, wo, l)
        xt = _ffn(xt, g_ffn2, w2i, w2o, g_final, l, l == DEPTH - 1)
    return xt.reshape(batch, seq, d)
```

```python
import functools
import math

import jax
import jax.numpy as jnp
import numpy as np
from jax import lax
from jax.experimental import pallas as pl
from jax.experimental.pallas import tpu as pltpu

D_MODEL = 1024
DEPTH = 4
RET_HEADS = 4
RET_DK = 128
RET_DV = 128
FOX_HEADS = 8
FOX_DH = 64
D_FF = 2816
ROPE_BASE = 10000.0
EPS = 1e-6

RET_W = RET_HEADS * RET_DK
FOX_W = FOX_HEADS * FOX_DH
FF_OFF = 4 * RET_W + 3 * FOX_W
MAIN_COLS = 5 * RET_W + 2 * D_MODEL
GATE_OFF = 5 * RET_W

LANES = 128
MXU_COLS = 256
BF16_ROWS = 16
VMEM_LIMIT_BYTES = 56 * 1024 * 1024

FFN_TM = 1024
FFN_SUB = 512
MIX_FFN_TM = 512
MIX_FFN_SUB = 512
FFN_CH = MXU_COLS
PROJ_TM = 1024
PROJ_SUB = 512
RET_C = 256
RET_TT = 512
FOX_TQ = 512
FOX_TK = 256
FOX_GROUP = 8
NEG = -1e30
LOG2E = math.log2(math.e)

AUG_PARTS = 3
AUG_SPAN = 2 * AUG_PARTS

BF16 = jnp.bfloat16
F32 = jnp.float32


def _dot(a, b):
    return jnp.dot(a, b, preferred_element_type=F32)


def _dot_nt(a, b):
    return lax.dot_general(a, b, (((1,), (1,)), ((), ())), preferred_element_type=F32)


def _dot_tn(a, b):
    return lax.dot_general(a, b, (((0,), (0,)), ((), ())), preferred_element_type=F32)


def _rms(x, g):
    ms = jnp.mean(x * x, axis=-1, keepdims=True)
    return x * lax.rsqrt(ms + EPS) * g


def _split3(x):
    hi = x.astype(BF16)
    r = x - hi.astype(F32)
    mid = r.astype(BF16)
    lo = (r - mid.astype(F32)).astype(BF16)
    return hi, mid, lo


def _resident(shape):
    nd = len(shape)
    return pl.BlockSpec(shape, lambda *_: (0,) * nd, pipeline_mode=pl.Buffered(1))


def _layer(shape, layer):
    nd = len(shape)
    return pl.BlockSpec((None,) + tuple(shape), lambda *_: (layer,) + (0,) * nd,
                        pipeline_mode=pl.Buffered(1))


def _params(*sem):
    return pltpu.CompilerParams(dimension_semantics=sem, vmem_limit_bytes=VMEM_LIMIT_BYTES)


def _ffn_kernel(*refs, tm, sub_rows, mix, final_norm):
    if mix:
        (x_ref, ret_ref, fox_ref, gr_ref, gf_ref, wor_ref, wof_ref, wmix_ref,
         g_ref, win_ref, wout_ref, gfin_ref, o_ref, acc_ref) = refs
    else:
        x_ref, g_ref, win_ref, wout_ref, gfin_ref, o_ref, acc_ref = refs
    for sub in range(tm // sub_rows):
        rows = slice(sub * sub_rows, (sub + 1) * sub_rows)
        x = x_ref[rows, :]
        if mix:
            y_ret = _dot(ret_ref[rows, :], wor_ref[...])
            y_fox = _dot(fox_ref[rows, :], wof_ref[...])
            merged = gr_ref[rows, :].astype(F32) * y_ret + gf_ref[rows, :].astype(F32) * y_fox
            x = x + _dot(merged.astype(BF16), wmix_ref[...])
        xn = _rms(x, g_ref[...]).astype(BF16)
        for c in range(D_FF // FFN_CH):
            lo = c * FFN_CH
            a = _dot(xn, win_ref[:, lo:lo + FFN_CH])
            b = _dot(xn, win_ref[:, D_FF + lo:D_FF + lo + FFN_CH])
            h = (a * jax.nn.sigmoid(a) * b).astype(BF16)
            y = _dot(h, wout_ref[lo:lo + FFN_CH, :])
            if c == 0:
                acc_ref[rows, :] = y
            else:
                acc_ref[rows, :] += y
        out = x + 0.5 * acc_ref[rows, :]
        if final_norm:
            out = _rms(out, gfin_ref[...])
        o_ref[rows, :] = out


def _ffn(x, g, w_in, w_out, g_final, layer, final_norm, mixer=None):
    t = x.shape[0]
    tm, sub_rows = (MIX_FFN_TM, MIX_FFN_SUB) if mixer is not None else (FFN_TM, FFN_SUB)

    def tok(width):
        return pl.BlockSpec((tm, width), lambda i: (i, 0))

    ffn_specs = [_layer((1, D_MODEL), layer), _layer((D_MODEL, 2 * D_FF), layer),
                 _layer((D_FF, D_MODEL), layer), _resident((1, D_MODEL))]
    ffn_args = (g, w_in, w_out, g_final)
    if mixer is not None:
        mix_specs = [tok(RET_W), tok(FOX_W), tok(D_MODEL), tok(D_MODEL),
                     _layer((RET_W, D_MODEL), layer), _layer((FOX_W, D_MODEL), layer),
                     _layer((D_MODEL, D_MODEL), layer)]
        in_specs, args = [tok(D_MODEL)] + mix_specs + ffn_specs, (x,) + tuple(mixer) + ffn_args
    else:
        in_specs, args = [tok(D_MODEL)] + ffn_specs, (x,) + ffn_args
    return pl.pallas_call(
        functools.partial(_ffn_kernel, tm=tm, sub_rows=sub_rows, mix=mixer is not None,
                          final_norm=final_norm),
        grid=(t // tm,),
        in_specs=in_specs,
        out_specs=tok(D_MODEL),
        out_shape=jax.ShapeDtypeStruct((t, D_MODEL), F32),
        scratch_shapes=[pltpu.VMEM((tm, D_MODEL), F32)],
        compiler_params=_params("arbitrary"),
        name="mix_ffn" if mixer is not None else "ffn",
    )(*args)


def _proj_kernel(x_ref, g_ref, w_ref, wqvf_ref, brow_ref, cos_ref, sin_ref,
                 rq_ref, rk_ref, rv_ref, rg_ref, fk_ref, gr_ref, gf_ref,
                 fqt_ref, fvt_ref, kaug_ref, qaug_ref, carry):
    @pl.when(pl.program_id(1) == 0)
    def _():
        carry[...] = jnp.zeros_like(carry)

    for sub in range(PROJ_TM // PROJ_SUB):
        _proj_rows(slice(sub * PROJ_SUB, (sub + 1) * PROJ_SUB),
                   x_ref, g_ref, w_ref, wqvf_ref, brow_ref, cos_ref, sin_ref,
                   rq_ref, rk_ref, rv_ref, rg_ref, fk_ref, gr_ref, gf_ref,
                   fqt_ref, fvt_ref, kaug_ref, qaug_ref, carry)


def _proj_rows(rows, x_ref, g_ref, w_ref, wqvf_ref, brow_ref, cos_ref, sin_ref,
               rq_ref, rk_ref, rv_ref, rg_ref, fk_ref, gr_ref, gf_ref,
               fqt_ref, fvt_ref, kaug_ref, qaug_ref, carry):
    tm = PROJ_SUB
    h = _rms(x_ref[rows, :], g_ref[...]).astype(BF16)

    qvf = _dot_nt(wqvf_ref[...], h)
    fqt_ref[0, :, rows] = (qvf[0:FOX_W] * (FOX_DH ** -0.5 * LOG2E)).astype(BF16)
    fvt_ref[0, :, rows] = qvf[FOX_W:2 * FOX_W].astype(BF16)

    c = jax.nn.log_sigmoid(qvf[2 * FOX_W:2 * FOX_W + BF16_ROWS] + brow_ref[...])
    lane = lax.broadcasted_iota(jnp.int32, (BF16_ROWS, tm), 1)
    shift = 1
    while shift < tm:
        c = c + jnp.where(lane >= shift, pltpu.roll(c, shift, 1), 0.0)
        shift *= 2
    c = c + carry[:, 0:1]
    carry[...] = jnp.broadcast_to(c[:, tm - 1:tm], carry.shape)

    parts = [part.astype(F32) for part in _split3(c * LOG2E)]
    srow = lax.broadcasted_iota(jnp.int32, (BF16_ROWS, tm), 0)
    q_blocks, k_blocks = [], []
    for pair in range(FOX_HEADS // 2):
        qb = jnp.zeros((BF16_ROWS, tm), F32)
        kb = jnp.zeros((BF16_ROWS, tm), F32)
        for j in range(2):
            base = j * AUG_SPAN
            for idx, part in enumerate(parts):
                src = jnp.broadcast_to(part[2 * pair + j:2 * pair + j + 1, :], (BF16_ROWS, tm))
                qb = jnp.where(srow == base + idx, src, qb)
                kb = jnp.where(srow == base + AUG_PARTS + idx, -src, kb)
            qb = jnp.where((srow >= base + AUG_PARTS) & (srow < base + AUG_SPAN), 1.0, qb)
            kb = jnp.where((srow >= base) & (srow < base + AUG_PARTS), 1.0, kb)
        q_blocks.append(qb)
        k_blocks.append(kb)
    pad = jnp.zeros((LANES - (FOX_HEADS // 2) * BF16_ROWS, tm), F32)
    qaug_ref[0, :, rows] = jnp.concatenate(q_blocks + [pad], axis=0).astype(BF16)
    kaug_ref[rows, :] = jnp.concatenate(k_blocks + [pad], axis=0).T.astype(BF16)

    def cols(off, width):
        return _dot(h, w_ref[:, off:off + width])

    cos = cos_ref[rows, :]
    sin = sin_ref[rows, :]

    def rope(t):
        parts = []
        for hd in range(RET_HEADS):
            th = t[:, hd * RET_DK:(hd + 1) * RET_DK]
            parts.append(th * cos + pltpu.roll(th, RET_DK // 2, 1) * sin)
        return jnp.concatenate(parts, axis=-1)

    rq_ref[rows, :] = rope(cols(0, RET_W)).astype(BF16)
    rk_ref[rows, :] = (rope(cols(RET_W, RET_W)) * (RET_DK ** -0.5)).astype(BF16)
    rv_ref[rows, :] = cols(2 * RET_W, RET_W).astype(BF16)
    rg_ref[rows, :] = cols(3 * RET_W, RET_W).astype(BF16)
    fk_ref[rows, :] = cols(4 * RET_W, FOX_W).astype(BF16)
    gr_ref[rows, :] = jax.nn.sigmoid(cols(GATE_OFF, D_MODEL)).astype(BF16)
    gf_ref[rows, :] = jax.nn.sigmoid(cols(GATE_OFF + D_MODEL, D_MODEL)).astype(BF16)


def _proj(x, g, w_main, wqvf_t, brow, cos, sin, layer, batch, seq):
    t = x.shape[0]
    tm = PROJ_TM
    ns = seq // tm

    def tok(width):
        return pl.BlockSpec((tm, width), lambda b, s: (b * ns + s, 0))

    def feat(rows):
        return pl.BlockSpec((1, rows, tm), lambda b, s: (b, 0, s))

    pos = pl.BlockSpec((tm, RET_DK), lambda b, s: (s, 0))
    out_shapes = [jax.ShapeDtypeStruct((t, RET_W), BF16)] * 4 + [
        jax.ShapeDtypeStruct((t, FOX_W), BF16)] + [
        jax.ShapeDtypeStruct((t, D_MODEL), BF16)] * 2 + [
        jax.ShapeDtypeStruct((batch, FOX_W, seq), BF16)] * 2 + [
        jax.ShapeDtypeStruct((t, LANES), BF16),
        jax.ShapeDtypeStruct((batch, LANES, seq), BF16)]
    out_specs = [tok(RET_W)] * 4 + [tok(FOX_W)] + [tok(D_MODEL)] * 2 + [
        feat(FOX_W), feat(FOX_W), tok(LANES), feat(LANES)]
    return pl.pallas_call(
        _proj_kernel,
        grid=(batch, ns),
        in_specs=[tok(D_MODEL), _layer((1, D_MODEL), layer), _layer((D_MODEL, MAIN_COLS), layer),
                  _layer((2 * FOX_W + BF16_ROWS, D_MODEL), layer),
                  _layer((BF16_ROWS, 1), layer), pos, pos],
        out_specs=out_specs,
        out_shape=out_shapes,
        scratch_shapes=[pltpu.VMEM((BF16_ROWS, LANES), F32)],
        compiler_params=_params("arbitrary", "arbitrary"),
        name="mix_proj",
    )(x, g, w_main, wqvf_t, brow, cos, sin)


def _ret_kernel(q_ref, k_ref, v_ref, rg_ref, rn_ref, o_ref, state_ref, decay_ref, xi_ref, zeta_ref):
    c = RET_C

    @pl.when((pl.program_id(0) == 0) & (pl.program_id(1) == 0))
    def _():
        row = lax.broadcasted_iota(jnp.int32, (c, c), 0)
        col = lax.broadcasted_iota(jnp.int32, (c, c), 1)
        diff = (row - col).astype(F32)
        pos = lax.broadcasted_iota(jnp.int32, (c, RET_DV), 0).astype(F32)
        for hd in range(RET_HEADS):
            log_gamma = float(np.log1p(-np.exp2(-5.0 - hd)))
            decay_ref[hd] = jnp.where(diff >= 0, jnp.exp(log_gamma * jnp.maximum(diff, 0.0)), 0.0)
            xi_ref[hd] = jnp.exp(log_gamma * (pos + 1.0))
            zeta_ref[hd] = jnp.exp(log_gamma * (c - 1.0 - pos))

    @pl.when(pl.program_id(1) == 0)
    def _():
        state_ref[...] = jnp.zeros_like(state_ref)

    for ch in range(RET_TT // c):
        rows = slice(ch * c, (ch + 1) * c)
        for hd in range(RET_HEADS):
            log_gamma = float(np.log1p(-np.exp2(-5.0 - hd)))
            sl = slice(hd * RET_DK, (hd + 1) * RET_DK)
            q = q_ref[rows, sl]
            k = k_ref[rows, sl]
            v = v_ref[rows, sl]
            scores = _dot_nt(q, k) * decay_ref[hd]
            inner = _dot(scores.astype(BF16), v)
            state = state_ref[hd]
            cross = _dot(q, state.astype(BF16)) * xi_ref[hd]
            vz = (v.astype(F32) * zeta_ref[hd]).astype(BF16)
            state_ref[hd] = float(np.exp(log_gamma * c)) * state + _dot_tn(k, vz)
            ret = inner + cross
            ret = ret * lax.rsqrt(jnp.mean(ret * ret, axis=-1, keepdims=True) + EPS)
            gate = jax.nn.silu(rg_ref[rows, sl].astype(F32))
            o_ref[rows, sl] = (gate * (ret * rn_ref[:, sl])).astype(BF16)


def _retention(rq, rk, rv, rg, ret_norm, layer, batch, seq):
    t = rq.shape[0]
    ns = seq // RET_TT
    tile = pl.BlockSpec((RET_TT, RET_W), lambda b, s: (b * ns + s, 0))
    return pl.pallas_call(
        _ret_kernel,
        grid=(batch, ns),
        in_specs=[tile, tile, tile, tile, _layer((1, RET_W), layer)],
        out_specs=tile,
        out_shape=jax.ShapeDtypeStruct((t, RET_W), BF16),
        scratch_shapes=[pltpu.VMEM((RET_HEADS, RET_DK, RET_DV), F32),
                        pltpu.VMEM((RET_HEADS, RET_C, RET_C), F32),
                        pltpu.VMEM((RET_HEADS, RET_C, RET_DV), F32),
                        pltpu.VMEM((RET_HEADS, RET_C, RET_DV), F32)],
        compiler_params=_params("arbitrary", "arbitrary"),
        name="retention",
    )(rq, rk, rv, rg, ret_norm)


def _fox_kernel(qt_ref, qaug_ref, k_ref, kaug_ref, vt_ref, o_ref,
                w_ref, s_ref, s2_ref, c_ref, c2_ref, p_ref, a_ref, m_ref, acc_ref):
    tq, tk = FOX_TQ, FOX_TK
    first_head = pl.program_id(1) * FOX_GROUP
    qi = pl.program_id(2)
    rowi = lax.broadcasted_iota(jnp.int32, (LANES, tq), 0)
    qaug = qaug_ref[0]
    for j in range(FOX_GROUP):
        qpair = qt_ref[0, (j // 2) * LANES:(j // 2 + 1) * LANES, :]
        zero = jnp.zeros_like(qpair)
        lo = (first_head + j) // 2 * BF16_ROWS + (j % 2) * AUG_SPAN
        top = jnp.where((rowi >= (j % 2) * FOX_DH) & (rowi < (j % 2 + 1) * FOX_DH), qpair, zero)
        bot = jnp.where((rowi >= lo) & (rowi < lo + AUG_SPAN), qaug, zero)
        w_ref[j, 0:LANES, :] = top
        w_ref[j, LANES:2 * LANES, :] = bot
    m_ref[...] = jnp.full_like(m_ref, NEG)
    acc_ref[...] = jnp.zeros_like(acc_ref)
    p_ref[...] = jnp.zeros_like(p_ref)
    a_ref[...] = jnp.ones_like(a_ref)

    def rows_of(kv):
        return pl.ds(pl.multiple_of(kv * tk, tk), tk)

    def score_head(j, kv, dst):
        dst_s, dst_c = dst
        rows = rows_of(kv)
        kblk = jnp.concatenate(
            [k_ref[rows, (j // 2) * LANES:(j // 2 + 1) * LANES], kaug_ref[rows, :]], axis=1)
        s = _dot(kblk, w_ref[j])
        dst_s[j] = s
        dst_c[j] = jnp.max(s, axis=0, keepdims=True)

    def softmax_head(j, src, diag):
        src_s, src_c = src
        s = src_s[j]
        if diag is not None:
            r = lax.broadcasted_iota(jnp.int32, (tk, tq), 0)
            cidx = lax.broadcasted_iota(jnp.int32, (tk, tq), 1)
            s = jnp.where(r + diag * tk <= cidx, s, NEG)
            cmax = jnp.max(s, axis=0, keepdims=True)
        else:
            cmax = src_c[j]
        m_old = m_ref[j]
        m_new = jnp.maximum(m_old, cmax)
        alpha = jnp.exp2(m_old - m_new)
        p = jnp.exp2(s - m_new)
        m_ref[j] = m_new
        a_ref[j] = alpha
        p_ref[j] = p.astype(BF16)

    def value_head(j, kv):
        rows = rows_of(kv)
        ones = jnp.ones((BF16_ROWS, tk), BF16)
        vt = jnp.concatenate([vt_ref[0, j * FOX_DH:(j + 1) * FOX_DH, rows], ones], axis=0)
        acc_ref[j] = a_ref[j] * acc_ref[j] + _dot(vt, p_ref[j])

    def score_stage(kv, dst):
        for j in range(FOX_GROUP):
            score_head(j, kv, dst)

    def softmax_stage(src, diag):
        for j in range(FOX_GROUP):
            softmax_head(j, src, diag)

    def value_stage(kv):
        for j in range(FOX_GROUP):
            value_head(j, kv)

    def step(i, src, dst, diag=None):
        prev = jnp.maximum(i - 1, 0)
        for j in range(FOX_GROUP):
            score_head(j, i + 1, dst)
            value_head(j, prev)
            softmax_head(j, src, diag)

    buf_a = (s_ref, c_ref)
    buf_b = (s2_ref, c2_ref)

    def body(h, carry):
        step(2 * h, buf_a, buf_b)
        step(2 * h + 1, buf_b, buf_a)
        return carry

    assert tq == 2 * tk
    first_diag = 2 * qi
    score_stage(0, buf_a)
    lax.fori_loop(0, qi, body, 0)
    step(first_diag, buf_a, buf_b, diag=0)

    @pl.when(qi >= 0)
    def _():
        value_stage(first_diag)
        softmax_stage(buf_b, 1)

    @pl.when(qi >= 0)
    def _():
        value_stage(first_diag + 1)
    out_t = jnp.concatenate(
        [acc_ref[j, 0:FOX_DH, :] / acc_ref[j, FOX_DH:FOX_DH + 1, :] for j in range(FOX_GROUP)], axis=0)
    o_ref[...] = out_t.T.astype(BF16)


def _fox(fqt, qaug, fk, kaug, fvt, batch, seq):
    t = fk.shape[0]
    tq, tk = FOX_TQ, FOX_TK
    nq = seq // tq
    gw = FOX_GROUP * FOX_DH
    return pl.pallas_call(
        _fox_kernel,
        grid=(batch, FOX_HEADS // FOX_GROUP, nq),
        in_specs=[
            pl.BlockSpec((1, gw, tq), lambda b, g, i: (b, g, i)),
            pl.BlockSpec((1, LANES, tq), lambda b, g, i: (b, 0, i)),
            pl.BlockSpec((seq, gw), lambda b, g, i: (b, g)),
            pl.BlockSpec((seq, LANES), lambda b, g, i: (b, 0)),
            pl.BlockSpec((1, gw, seq), lambda b, g, i: (b, g, 0)),
        ],
        out_specs=pl.BlockSpec((tq, gw), lambda b, g, i: (b * nq + i, g)),
        out_shape=jax.ShapeDtypeStruct((t, FOX_W), BF16),
        scratch_shapes=[pltpu.VMEM((FOX_GROUP, 2 * LANES, tq), BF16),
                        pltpu.VMEM((FOX_GROUP, tk, tq), F32),
                        pltpu.VMEM((FOX_GROUP, tk, tq), F32),
                        pltpu.VMEM((FOX_GROUP, 1, tq), F32),
                        pltpu.VMEM((FOX_GROUP, 1, tq), F32),
                        pltpu.VMEM((FOX_GROUP, tk, tq), BF16),
                        pltpu.VMEM((FOX_GROUP, 1, tq), F32),
                        pltpu.VMEM((FOX_GROUP, 1, tq), F32),
                        pltpu.VMEM((FOX_GROUP, FOX_DH + BF16_ROWS, tq), F32)],
        compiler_params=_params("arbitrary", "arbitrary", "arbitrary"),
        name="fox_attn",
    )(fqt, qaug, fk, kaug, fvt)


def _rope_tables(seq):
    d = RET_DK
    inv = jnp.power(ROPE_BASE, -jnp.arange(0, d, 2, dtype=F32) / d)
    ang = jnp.arange(seq, dtype=F32)[:, None] * inv[None, :]
    cos, sin = jnp.cos(ang), jnp.sin(ang)
    return jnp.concatenate([cos, cos], axis=-1), jnp.concatenate([-sin, sin], axis=-1)


def kernel(x, norm_ffn1, w_ffn1_in, w_ffn1_out, norm_mix, w_in, b_forget, ret_norm,
           w_o_ret, w_o_fox, w_out, norm_ffn2, w_ffn2_in, w_ffn2_out, norm_final):
    batch, seq, d = x.shape
    assert d == D_MODEL and all(seq % tile == 0 for tile in (PROJ_TM, RET_TT, FOX_TQ))
    assert all((batch * seq) % tile == 0 for tile in (FFN_TM, MIX_FFN_TM))
    xt = x.reshape(batch * seq, d)

    w1i = w_ffn1_in.astype(BF16)
    w1o = w_ffn1_out.astype(BF16)
    w2i = w_ffn2_in.astype(BF16)
    w2o = w_ffn2_out.astype(BF16)
    fq_off = 4 * RET_W
    w_main = jnp.concatenate(
        [w_in[..., :fq_off], w_in[..., fq_off + FOX_W:fq_off + 2 * FOX_W],
         w_in[..., FF_OFF + FOX_HEADS:]], axis=-1).astype(BF16)
    wqvf_t = jnp.swapaxes(jnp.concatenate(
        [w_in[..., fq_off:fq_off + FOX_W], w_in[..., fq_off + 2 * FOX_W:FF_OFF + FOX_HEADS],
         jnp.zeros((DEPTH, D_MODEL, BF16_ROWS - FOX_HEADS), w_in.dtype)], axis=-1).astype(BF16), 1, 2)
    brow = jnp.pad(b_forget, ((0, 0), (0, BF16_ROWS - FOX_HEADS)))[:, :, None]
    wor = w_o_ret.astype(BF16)
    wof = w_o_fox.astype(BF16)
    wo = w_out.astype(BF16)
    cos, sin = _rope_tables(seq)
    g_final = norm_final[None, :]
    g_ffn1 = norm_ffn1[:, None, :]
    g_mix = norm_mix[:, None, :]
    g_ffn2 = norm_ffn2[:, None, :]
    g_ret = ret_norm[:, None, :]

    for l in range(DEPTH):
        xt = _ffn(xt, g_ffn1, w1i, w1o, g_final, l, False)
        rq, rk, rv, rg, fk, gr, gf, fqt, fvt, kaug, qaug = _proj(
            xt, g_mix, w_main, wqvf_t, brow, cos, sin, l, batch, seq)
        ret = _retention(rq, rk, rv, rg, g_ret, l, batch, seq)
        fox = _fox(fqt, qaug, fk, kaug, fvt, batch, seq)
        xt = _ffn(xt, g_ffn2, w2i, w2o, g_final, l, l == DEPTH - 1,
                  mixer=(ret, fox, gr, gf, wor, wof, wo))
    return xt.reshape(batch, seq, d)
```

```python
import functools
import math

import jax
import jax.numpy as jnp
import numpy as np
from jax import lax
from jax.experimental import pallas as pl
from jax.experimental.pallas import tpu as pltpu

D_MODEL = 1024
DEPTH = 4
RET_HEADS = 4
RET_DK = 128
RET_DV = 128
FOX_HEADS = 8
FOX_DH = 64
D_FF = 2816
ROPE_BASE = 10000.0
EPS = 1e-6

RET_W = RET_HEADS * RET_DK
FOX_W = FOX_HEADS * FOX_DH
FF_OFF = 4 * RET_W + 3 * FOX_W
MAIN_COLS = 5 * RET_W + 2 * D_MODEL
GATE_OFF = 5 * RET_W

LANES = 128
MXU_COLS = 256
BF16_ROWS = 16
VMEM_LIMIT_BYTES = 56 * 1024 * 1024

FFN_TM = 1024
FFN_SUB = 512
MIX_FFN_TM = 512
MIX_FFN_SUB = 512
FFN_CH = MXU_COLS
PROJ_TM = 1024
PROJ_SUB = 512
RET_C = 256
FOX_TQ = 512
FOX_TK = 256
FOX_GROUP = 8
NEG = -1e30
LOG2E = math.log2(math.e)

AUG_PARTS = 3
AUG_SPAN = 2 * AUG_PARTS

BF16 = jnp.bfloat16
F32 = jnp.float32


def _dot(a, b):
    return jnp.dot(a, b, preferred_element_type=F32)


def _dot_nt(a, b):
    return lax.dot_general(a, b, (((1,), (1,)), ((), ())), preferred_element_type=F32)


def _dot_tn(a, b):
    return lax.dot_general(a, b, (((0,), (0,)), ((), ())), preferred_element_type=F32)


def _rms(x, g):
    ms = jnp.mean(x * x, axis=-1, keepdims=True)
    return x * lax.rsqrt(ms + EPS) * g


def _split3(x):
    hi = x.astype(BF16)
    r = x - hi.astype(F32)
    mid = r.astype(BF16)
    lo = (r - mid.astype(F32)).astype(BF16)
    return hi, mid, lo


def _resident(shape):
    nd = len(shape)
    return pl.BlockSpec(shape, lambda *_: (0,) * nd, pipeline_mode=pl.Buffered(1))


def _layer(shape, layer):
    nd = len(shape)
    return pl.BlockSpec((None,) + tuple(shape), lambda *_: (layer,) + (0,) * nd,
                        pipeline_mode=pl.Buffered(1))


def _params(*sem):
    return pltpu.CompilerParams(dimension_semantics=sem, vmem_limit_bytes=VMEM_LIMIT_BYTES)


def _ffn_kernel(*refs, tm, sub_rows, mix, final_norm):
    if mix:
        (x_ref, ret_ref, fox_ref, gr_ref, gf_ref, wor_ref, wof_ref, wmix_ref,
         g_ref, win_ref, wout_ref, gfin_ref, o_ref, acc_ref) = refs
    else:
        x_ref, g_ref, win_ref, wout_ref, gfin_ref, o_ref, acc_ref = refs
    for sub in range(tm // sub_rows):
        rows = slice(sub * sub_rows, (sub + 1) * sub_rows)
        x = x_ref[rows, :]
        if mix:
            y_ret = _dot(ret_ref[rows, :], wor_ref[...])
            y_fox = _dot(fox_ref[rows, :], wof_ref[...])
            merged = gr_ref[rows, :].astype(F32) * y_ret + gf_ref[rows, :].astype(F32) * y_fox
            x = x + _dot(merged.astype(BF16), wmix_ref[...])
        xn = _rms(x, g_ref[...]).astype(BF16)
        for c in range(D_FF // FFN_CH):
            lo = c * FFN_CH
            a = _dot(xn, win_ref[:, lo:lo + FFN_CH])
            b = _dot(xn, win_ref[:, D_FF + lo:D_FF + lo + FFN_CH])
            h = (a * jax.nn.sigmoid(a) * b).astype(BF16)
            y = _dot(h, wout_ref[lo:lo + FFN_CH, :])
            if c == 0:
                acc_ref[rows, :] = y
            else:
                acc_ref[rows, :] += y
        out = x + 0.5 * acc_ref[rows, :]
        if final_norm:
            out = _rms(out, gfin_ref[...])
        o_ref[rows, :] = out


def _ffn(x, g, w_in, w_out, g_final, layer, final_norm, mixer=None):
    t = x.shape[0]
    tm, sub_rows = (MIX_FFN_TM, MIX_FFN_SUB) if mixer is not None else (FFN_TM, FFN_SUB)

    def tok(width):
        return pl.BlockSpec((tm, width), lambda i: (i, 0))

    ffn_specs = [_layer((1, D_MODEL), layer), _layer((D_MODEL, 2 * D_FF), layer),
                 _layer((D_FF, D_MODEL), layer), _resident((1, D_MODEL))]
    ffn_args = (g, w_in, w_out, g_final)
    if mixer is not None:
        mix_specs = [tok(RET_W), tok(FOX_W), tok(D_MODEL), tok(D_MODEL),
                     _layer((RET_W, D_MODEL), layer), _layer((FOX_W, D_MODEL), layer),
                     _layer((D_MODEL, D_MODEL), layer)]
        in_specs, args = [tok(D_MODEL)] + mix_specs + ffn_specs, (x,) + tuple(mixer) + ffn_args
    else:
        in_specs, args = [tok(D_MODEL)] + ffn_specs, (x,) + ffn_args
    return pl.pallas_call(
        functools.partial(_ffn_kernel, tm=tm, sub_rows=sub_rows, mix=mixer is not None,
                          final_norm=final_norm),
        grid=(t // tm,),
        in_specs=in_specs,
        out_specs=tok(D_MODEL),
        out_shape=jax.ShapeDtypeStruct((t, D_MODEL), F32),
        scratch_shapes=[pltpu.VMEM((tm, D_MODEL), F32)],
        compiler_params=_params("arbitrary"),
        name="mix_ffn" if mixer is not None else "ffn",
    )(*args)


def _proj_kernel(x_ref, g_ref, w_ref, wqvf_ref, brow_ref, cos_ref, sin_ref, rn_ref,
                 ret_ref, fk_ref, gr_ref, gf_ref, fqt_ref, fvt_ref, kaug_ref, qaug_ref,
                 carry, rq_ref, rk_ref, rv_ref, rg_ref, state_ref, decay_ref, xi_ref, zeta_ref):
    @pl.when((pl.program_id(0) == 0) & (pl.program_id(1) == 0))
    def _():
        _retention_tables(decay_ref, xi_ref, zeta_ref)

    @pl.when(pl.program_id(1) == 0)
    def _():
        carry[...] = jnp.zeros_like(carry)
        state_ref[...] = jnp.zeros_like(state_ref)

    for sub in range(PROJ_TM // PROJ_SUB):
        _proj_rows(slice(sub * PROJ_SUB, (sub + 1) * PROJ_SUB),
                   x_ref, g_ref, w_ref, wqvf_ref, brow_ref, cos_ref, sin_ref,
                   rq_ref, rk_ref, rv_ref, rg_ref, fk_ref, gr_ref, gf_ref,
                   fqt_ref, fvt_ref, kaug_ref, qaug_ref, carry)
        _retention_rows(sub * PROJ_SUB, PROJ_SUB, rq_ref, rk_ref, rv_ref, rg_ref, rn_ref, ret_ref,
                        state_ref, decay_ref, xi_ref, zeta_ref)


def _proj_rows(rows, x_ref, g_ref, w_ref, wqvf_ref, brow_ref, cos_ref, sin_ref,
               rq_ref, rk_ref, rv_ref, rg_ref, fk_ref, gr_ref, gf_ref,
               fqt_ref, fvt_ref, kaug_ref, qaug_ref, carry):
    tm = PROJ_SUB
    h = _rms(x_ref[rows, :], g_ref[...]).astype(BF16)

    qvf = _dot_nt(wqvf_ref[...], h)
    fqt_ref[0, :, rows] = (qvf[0:FOX_W] * (FOX_DH ** -0.5 * LOG2E)).astype(BF16)
    fvt_ref[0, :, rows] = qvf[FOX_W:2 * FOX_W].astype(BF16)

    c = jax.nn.log_sigmoid(qvf[2 * FOX_W:2 * FOX_W + BF16_ROWS] + brow_ref[...])
    lane = lax.broadcasted_iota(jnp.int32, (BF16_ROWS, tm), 1)
    shift = 1
    while shift < tm:
        c = c + jnp.where(lane >= shift, pltpu.roll(c, shift, 1), 0.0)
        shift *= 2
    c = c + carry[:, 0:1]
    carry[...] = jnp.broadcast_to(c[:, tm - 1:tm], carry.shape)

    parts = [part.astype(F32) for part in _split3(c * LOG2E)]
    srow = lax.broadcasted_iota(jnp.int32, (BF16_ROWS, tm), 0)
    q_blocks, k_blocks = [], []
    for pair in range(FOX_HEADS // 2):
        qb = jnp.zeros((BF16_ROWS, tm), F32)
        kb = jnp.zeros((BF16_ROWS, tm), F32)
        for j in range(2):
            base = j * AUG_SPAN
            for idx, part in enumerate(parts):
                src = jnp.broadcast_to(part[2 * pair + j:2 * pair + j + 1, :], (BF16_ROWS, tm))
                qb = jnp.where(srow == base + idx, src, qb)
                kb = jnp.where(srow == base + AUG_PARTS + idx, -src, kb)
            qb = jnp.where((srow >= base + AUG_PARTS) & (srow < base + AUG_SPAN), 1.0, qb)
            kb = jnp.where((srow >= base) & (srow < base + AUG_PARTS), 1.0, kb)
        q_blocks.append(qb)
        k_blocks.append(kb)
    pad = jnp.zeros((LANES - (FOX_HEADS // 2) * BF16_ROWS, tm), F32)
    qaug_ref[0, :, rows] = jnp.concatenate(q_blocks + [pad], axis=0).astype(BF16)
    kaug_ref[rows, :] = jnp.concatenate(k_blocks + [pad], axis=0).T.astype(BF16)

    def cols(off, width):
        return _dot(h, w_ref[:, off:off + width])

    cos = cos_ref[rows, :]
    sin = sin_ref[rows, :]

    def rope(t):
        parts = []
        for hd in range(RET_HEADS):
            th = t[:, hd * RET_DK:(hd + 1) * RET_DK]
            parts.append(th * cos + pltpu.roll(th, RET_DK // 2, 1) * sin)
        return jnp.concatenate(parts, axis=-1)

    rq_ref[rows, :] = rope(cols(0, RET_W)).astype(BF16)
    rk_ref[rows, :] = (rope(cols(RET_W, RET_W)) * (RET_DK ** -0.5)).astype(BF16)
    rv_ref[rows, :] = cols(2 * RET_W, RET_W).astype(BF16)
    rg_ref[rows, :] = cols(3 * RET_W, RET_W).astype(BF16)
    fk_ref[rows, :] = cols(4 * RET_W, FOX_W).astype(BF16)
    gr_ref[rows, :] = jax.nn.sigmoid(cols(GATE_OFF, D_MODEL)).astype(BF16)
    gf_ref[rows, :] = jax.nn.sigmoid(cols(GATE_OFF + D_MODEL, D_MODEL)).astype(BF16)


def _proj(x, g, w_main, wqvf_t, brow, cos, sin, ret_norm, layer, batch, seq):
    t = x.shape[0]
    tm = PROJ_TM
    ns = seq // tm

    def tok(width):
        return pl.BlockSpec((tm, width), lambda b, s: (b * ns + s, 0))

    def feat(rows):
        return pl.BlockSpec((1, rows, tm), lambda b, s: (b, 0, s))

    pos = pl.BlockSpec((tm, RET_DK), lambda b, s: (s, 0))
    out_shapes = [jax.ShapeDtypeStruct((t, RET_W), BF16),
                  jax.ShapeDtypeStruct((t, FOX_W), BF16)] + [
        jax.ShapeDtypeStruct((t, D_MODEL), BF16)] * 2 + [
        jax.ShapeDtypeStruct((batch, FOX_W, seq), BF16)] * 2 + [
        jax.ShapeDtypeStruct((t, LANES), BF16),
        jax.ShapeDtypeStruct((batch, LANES, seq), BF16)]
    out_specs = [tok(RET_W), tok(FOX_W)] + [tok(D_MODEL)] * 2 + [
        feat(FOX_W), feat(FOX_W), tok(LANES), feat(LANES)]
    return pl.pallas_call(
        _proj_kernel,
        grid=(batch, ns),
        in_specs=[tok(D_MODEL), _layer((1, D_MODEL), layer), _layer((D_MODEL, MAIN_COLS), layer),
                  _layer((2 * FOX_W + BF16_ROWS, D_MODEL), layer),
                  _layer((BF16_ROWS, 1), layer), pos, pos, _layer((1, RET_W), layer)],
        out_specs=out_specs,
        out_shape=out_shapes,
        scratch_shapes=[pltpu.VMEM((BF16_ROWS, LANES), F32)] + [
            pltpu.VMEM((tm, RET_W), BF16)] * 4 + [
            pltpu.VMEM((RET_HEADS, RET_DK, RET_DV), F32),
            pltpu.VMEM((RET_HEADS, RET_C, RET_C), F32),
            pltpu.VMEM((RET_HEADS, RET_C, RET_DV), F32),
            pltpu.VMEM((RET_HEADS, RET_C, RET_DV), F32)],
        compiler_params=_params("arbitrary", "arbitrary"),
        name="mix_proj",
    )(x, g, w_main, wqvf_t, brow, cos, sin, ret_norm)


def _retention_tables(decay_ref, xi_ref, zeta_ref):
    c = RET_C
    row = lax.broadcasted_iota(jnp.int32, (c, c), 0)
    col = lax.broadcasted_iota(jnp.int32, (c, c), 1)
    diff = (row - col).astype(F32)
    pos = lax.broadcasted_iota(jnp.int32, (c, RET_DV), 0).astype(F32)
    for hd in range(RET_HEADS):
        log_gamma = float(np.log1p(-np.exp2(-5.0 - hd)))
        decay_ref[hd] = jnp.where(diff >= 0, jnp.exp(log_gamma * jnp.maximum(diff, 0.0)), 0.0)
        xi_ref[hd] = jnp.exp(log_gamma * (pos + 1.0))
        zeta_ref[hd] = jnp.exp(log_gamma * (c - 1.0 - pos))


def _retention_rows(first_row, n_rows, q_ref, k_ref, v_ref, rg_ref, rn_ref, o_ref,
                    state_ref, decay_ref, xi_ref, zeta_ref):
    c = RET_C
    for ch in range(n_rows // c):
        rows = slice(first_row + ch * c, first_row + (ch + 1) * c)
        for hd in range(RET_HEADS):
            log_gamma = float(np.log1p(-np.exp2(-5.0 - hd)))
            sl = slice(hd * RET_DK, (hd + 1) * RET_DK)
            q = q_ref[rows, sl]
            k = k_ref[rows, sl]
            v = v_ref[rows, sl]
            scores = _dot_nt(q, k) * decay_ref[hd]
            inner = _dot(scores.astype(BF16), v)
            state = state_ref[hd]
            cross = _dot(q, state.astype(BF16)) * xi_ref[hd]
            vz = (v.astype(F32) * zeta_ref[hd]).astype(BF16)
            state_ref[hd] = float(np.exp(log_gamma * c)) * state + _dot_tn(k, vz)
            ret = inner + cross
            ret = ret * lax.rsqrt(jnp.mean(ret * ret, axis=-1, keepdims=True) + EPS)
            gate = jax.nn.silu(rg_ref[rows, sl].astype(F32))
            o_ref[rows, sl] = (gate * (ret * rn_ref[:, sl])).astype(BF16)


def _fox_kernel(qt_ref, qaug_ref, k_ref, kaug_ref, vt_ref, o_ref,
                w_ref, s_ref, s2_ref, c_ref, c2_ref, p_ref, a_ref, m_ref, acc_ref):
    tq, tk = FOX_TQ, FOX_TK
    first_head = pl.program_id(1) * FOX_GROUP
    qi = pl.program_id(2)
    rowi = lax.broadcasted_iota(jnp.int32, (LANES, tq), 0)
    qaug = qaug_ref[0]
    for j in range(FOX_GROUP):
        qpair = qt_ref[0, (j // 2) * LANES:(j // 2 + 1) * LANES, :]
        zero = jnp.zeros_like(qpair)
        lo = (first_head + j) // 2 * BF16_ROWS + (j % 2) * AUG_SPAN
        top = jnp.where((rowi >= (j % 2) * FOX_DH) & (rowi < (j % 2 + 1) * FOX_DH), qpair, zero)
        bot = jnp.where((rowi >= lo) & (rowi < lo + AUG_SPAN), qaug, zero)
        w_ref[j, 0:LANES, :] = top
        w_ref[j, LANES:2 * LANES, :] = bot
    m_ref[...] = jnp.full_like(m_ref, NEG)
    acc_ref[...] = jnp.zeros_like(acc_ref)
    p_ref[...] = jnp.zeros_like(p_ref)
    a_ref[...] = jnp.ones_like(a_ref)

    def rows_of(kv):
        return pl.ds(pl.multiple_of(kv * tk, tk), tk)

    def score_head(j, kv, dst):
        dst_s, dst_c = dst
        rows = rows_of(kv)
        kblk = jnp.concatenate(
            [k_ref[rows, (j // 2) * LANES:(j // 2 + 1) * LANES], kaug_ref[rows, :]], axis=1)
        s = _dot(kblk, w_ref[j])
        dst_s[j] = s
        dst_c[j] = jnp.max(s, axis=0, keepdims=True)

    def softmax_head(j, src, diag):
        src_s, src_c = src
        s = src_s[j]
        if diag is not None:
            r = lax.broadcasted_iota(jnp.int32, (tk, tq), 0)
            cidx = lax.broadcasted_iota(jnp.int32, (tk, tq), 1)
            s = jnp.where(r + diag * tk <= cidx, s, NEG)
            cmax = jnp.max(s, axis=0, keepdims=True)
        else:
            cmax = src_c[j]
        m_old = m_ref[j]
        m_new = jnp.maximum(m_old, cmax)
        alpha = jnp.exp2(m_old - m_new)
        p = jnp.exp2(s - m_new)
        m_ref[j] = m_new
        a_ref[j] = alpha
        p_ref[j] = p.astype(BF16)

    def value_head(j, kv):
        rows = rows_of(kv)
        ones = jnp.ones((BF16_ROWS, tk), BF16)
        vt = jnp.concatenate([vt_ref[0, j * FOX_DH:(j + 1) * FOX_DH, rows], ones], axis=0)
        acc_ref[j] = a_ref[j] * acc_ref[j] + _dot(vt, p_ref[j])

    def score_stage(kv, dst):
        for j in range(FOX_GROUP):
            score_head(j, kv, dst)

    def softmax_stage(src, diag):
        for j in range(FOX_GROUP):
            softmax_head(j, src, diag)

    def value_stage(kv):
        for j in range(FOX_GROUP):
            value_head(j, kv)

    def step(i, src, dst, diag=None):
        prev = jnp.maximum(i - 1, 0)
        for j in range(FOX_GROUP):
            score_head(j, i + 1, dst)
            value_head(j, prev)
            softmax_head(j, src, diag)

    buf_a = (s_ref, c_ref)
    buf_b = (s2_ref, c2_ref)

    def body(h, carry):
        step(2 * h, buf_a, buf_b)
        step(2 * h + 1, buf_b, buf_a)
        return carry

    assert tq == 2 * tk
    first_diag = 2 * qi
    score_stage(0, buf_a)
    lax.fori_loop(0, qi, body, 0)
    step(first_diag, buf_a, buf_b, diag=0)

    @pl.when(qi >= 0)
    def _():
        value_stage(first_diag)
        softmax_stage(buf_b, 1)

    @pl.when(qi >= 0)
    def _():
        value_stage(first_diag + 1)
    out_t = jnp.concatenate(
        [acc_ref[j, 0:FOX_DH, :] / acc_ref[j, FOX_DH:FOX_DH + 1, :] for j in range(FOX_GROUP)], axis=0)
    o_ref[...] = out_t.T.astype(BF16)


def _fox(fqt, qaug, fk, kaug, fvt, batch, seq):
    t = fk.shape[0]
    tq, tk = FOX_TQ, FOX_TK
    nq = seq // tq
    gw = FOX_GROUP * FOX_DH
    return pl.pallas_call(
        _fox_kernel,
        grid=(batch, FOX_HEADS // FOX_GROUP, nq),
        in_specs=[
            pl.BlockSpec((1, gw, tq), lambda b, g, i: (b, g, i)),
            pl.BlockSpec((1, LANES, tq), lambda b, g, i: (b, 0, i)),
            pl.BlockSpec((seq, gw), lambda b, g, i: (b, g)),
            pl.BlockSpec((seq, LANES), lambda b, g, i: (b, 0)),
            pl.BlockSpec((1, gw, seq), lambda b, g, i: (b, g, 0)),
        ],
        out_specs=pl.BlockSpec((tq, gw), lambda b, g, i: (b * nq + i, g)),
        out_shape=jax.ShapeDtypeStruct((t, FOX_W), BF16),
        scratch_shapes=[pltpu.VMEM((FOX_GROUP, 2 * LANES, tq), BF16),
                        pltpu.VMEM((FOX_GROUP, tk, tq), F32),
                        pltpu.VMEM((FOX_GROUP, tk, tq), F32),
                        pltpu.VMEM((FOX_GROUP, 1, tq), F32),
                        pltpu.VMEM((FOX_GROUP, 1, tq), F32),
                        pltpu.VMEM((FOX_GROUP, tk, tq), BF16),
                        pltpu.VMEM((FOX_GROUP, 1, tq), F32),
                        pltpu.VMEM((FOX_GROUP, 1, tq), F32),
                        pltpu.VMEM((FOX_GROUP, FOX_DH + BF16_ROWS, tq), F32)],
        compiler_params=_params("arbitrary", "arbitrary", "arbitrary"),
        name="fox_attn",
    )(fqt, qaug, fk, kaug, fvt)


def _rope_tables(seq):
    d = RET_DK
    inv = jnp.power(ROPE_BASE, -jnp.arange(0, d, 2, dtype=F32) / d)
    ang = jnp.arange(seq, dtype=F32)[:, None] * inv[None, :]
    cos, sin = jnp.cos(ang), jnp.sin(ang)
    return jnp.concatenate([cos, cos], axis=-1), jnp.concatenate([-sin, sin], axis=-1)


def kernel(x, norm_ffn1, w_ffn1_in, w_ffn1_out, norm_mix, w_in, b_forget, ret_norm,
           w_o_ret, w_o_fox, w_out, norm_ffn2, w_ffn2_in, w_ffn2_out, norm_final):
    batch, seq, d = x.shape
    assert d == D_MODEL and seq % PROJ_TM == 0 and seq % FOX_TQ == 0 and PROJ_SUB % RET_C == 0
    assert all((batch * seq) % tile == 0 for tile in (FFN_TM, MIX_FFN_TM))
    xt = x.reshape(batch * seq, d)

    w1i = w_ffn1_in.astype(BF16)
    w1o = w_ffn1_out.astype(BF16)
    w2i = w_ffn2_in.astype(BF16)
    w2o = w_ffn2_out.astype(BF16)
    fq_off = 4 * RET_W
    w_main = jnp.concatenate(
        [w_in[..., :fq_off], w_in[..., fq_off + FOX_W:fq_off + 2 * FOX_W],
         w_in[..., FF_OFF + FOX_HEADS:]], axis=-1).astype(BF16)
    wqvf_t = jnp.swapaxes(jnp.concatenate(
        [w_in[..., fq_off:fq_off + FOX_W], w_in[..., fq_off + 2 * FOX_W:FF_OFF + FOX_HEADS],
         jnp.zeros((DEPTH, D_MODEL, BF16_ROWS - FOX_HEADS), w_in.dtype)], axis=-1).astype(BF16), 1, 2)
    brow = jnp.pad(b_forget, ((0, 0), (0, BF16_ROWS - FOX_HEADS)))[:, :, None]
    wor = w_o_ret.astype(BF16)
    wof = w_o_fox.astype(BF16)
    wo = w_out.astype(BF16)
    cos, sin = _rope_tables(seq)
    g_final = norm_final[None, :]
    g_ffn1 = norm_ffn1[:, None, :]
    g_mix = norm_mix[:, None, :]
    g_ffn2 = norm_ffn2[:, None, :]
    g_ret = ret_norm[:, None, :]

    for l in range(DEPTH):
        xt = _ffn(xt, g_ffn1, w1i, w1o, g_final, l, False)
        ret, fk, gr, gf, fqt, fvt, kaug, qaug = _proj(
            xt, g_mix, w_main, wqvf_t, brow, cos, sin, g_ret, l, batch, seq)
        fox = _fox(fqt, qaug, fk, kaug, fvt, batch, seq)
        xt = _ffn(xt, g_ffn2, w2i, w2o, g_final, l, l == DEPTH - 1,
                  mixer=(ret, fox, gr, gf, wor, wof, wo))
    return xt.reshape(batch, seq, d)
```

```python
import functools
import math

import jax
import jax.numpy as jnp
import numpy as np
from jax import lax
from jax.experimental import pallas as pl
from jax.experimental.pallas import tpu as pltpu

D_MODEL = 1024
DEPTH = 4
RET_HEADS = 4
RET_DK = 128
RET_DV = 128
FOX_HEADS = 8
FOX_DH = 64
D_FF = 2816
ROPE_BASE = 10000.0
EPS = 1e-6

RET_W = RET_HEADS * RET_DK
FOX_W = FOX_HEADS * FOX_DH
FF_OFF = 4 * RET_W + 3 * FOX_W
MAIN_COLS = 4 * RET_W + 2 * D_MODEL
GATE_OFF = 4 * RET_W

LANES = 128
MXU_COLS = 256
BF16_ROWS = 16
VMEM_LIMIT_BYTES = 56 * 1024 * 1024

FFN_TM = 1024
FFN_SUB = 512
MIX_FFN_TM = 512
MIX_FFN_SUB = 512
FFN_CH = MXU_COLS
PROJ_TM = 1024
PROJ_SUB = 512
RET_C = 256
FOX_TQ = 512
FOX_TK = 256
FOX_GROUP = 8
NEG = -1e30
LOG2E = math.log2(math.e)

AUG_PARTS = 3
AUG_SPAN = 2 * AUG_PARTS

BF16 = jnp.bfloat16
F32 = jnp.float32


def _dot(a, b):
    return jnp.dot(a, b, preferred_element_type=F32)


def _dot_nt(a, b):
    return lax.dot_general(a, b, (((1,), (1,)), ((), ())), preferred_element_type=F32)


def _dot_tn(a, b):
    return lax.dot_general(a, b, (((0,), (0,)), ((), ())), preferred_element_type=F32)


def _rms(x, g):
    ms = jnp.mean(x * x, axis=-1, keepdims=True)
    return x * lax.rsqrt(ms + EPS) * g


def _split3(x):
    hi = x.astype(BF16)
    r = x - hi.astype(F32)
    mid = r.astype(BF16)
    lo = (r - mid.astype(F32)).astype(BF16)
    return hi, mid, lo


def _resident(shape):
    nd = len(shape)
    return pl.BlockSpec(shape, lambda *_: (0,) * nd, pipeline_mode=pl.Buffered(1))


def _layer(shape, layer):
    nd = len(shape)
    return pl.BlockSpec((None,) + tuple(shape), lambda *_: (layer,) + (0,) * nd,
                        pipeline_mode=pl.Buffered(1))


def _params(*sem):
    return pltpu.CompilerParams(dimension_semantics=sem, vmem_limit_bytes=VMEM_LIMIT_BYTES)


def _ffn_kernel(*refs, tm, sub_rows, mix, final_norm):
    if mix:
        (x_ref, ret_ref, fox_ref, gr_ref, gf_ref, wor_ref, wof_ref, wmix_ref,
         g_ref, win_ref, wout_ref, gfin_ref, o_ref, acc_ref) = refs
    else:
        x_ref, g_ref, win_ref, wout_ref, gfin_ref, o_ref, acc_ref = refs
    for sub in range(tm // sub_rows):
        rows = slice(sub * sub_rows, (sub + 1) * sub_rows)
        x = x_ref[rows, :]
        if mix:
            y_ret = _dot(ret_ref[rows, :], wor_ref[...])
            y_fox = _dot(fox_ref[rows, :], wof_ref[...])
            merged = gr_ref[rows, :].astype(F32) * y_ret + gf_ref[rows, :].astype(F32) * y_fox
            x = x + _dot(merged.astype(BF16), wmix_ref[...])
        xn = _rms(x, g_ref[...]).astype(BF16)
        for c in range(D_FF // FFN_CH):
            lo = c * FFN_CH
            a = _dot(xn, win_ref[:, lo:lo + FFN_CH])
            b = _dot(xn, win_ref[:, D_FF + lo:D_FF + lo + FFN_CH])
            h = (a * jax.nn.sigmoid(a) * b).astype(BF16)
            y = _dot(h, wout_ref[lo:lo + FFN_CH, :])
            if c == 0:
                acc_ref[rows, :] = y
            else:
                acc_ref[rows, :] += y
        out = x + 0.5 * acc_ref[rows, :]
        if final_norm:
            out = _rms(out, gfin_ref[...])
        o_ref[rows, :] = out


def _ffn(x, g, w_in, w_out, g_final, layer, final_norm, mixer=None):
    t = x.shape[0]
    tm, sub_rows = (MIX_FFN_TM, MIX_FFN_SUB) if mixer is not None else (FFN_TM, FFN_SUB)

    def tok(width):
        return pl.BlockSpec((tm, width), lambda i: (i, 0))

    ffn_specs = [_layer((1, D_MODEL), layer), _layer((D_MODEL, 2 * D_FF), layer),
                 _layer((D_FF, D_MODEL), layer), _resident((1, D_MODEL))]
    ffn_args = (g, w_in, w_out, g_final)
    if mixer is not None:
        mix_specs = [tok(RET_W), tok(FOX_W), tok(D_MODEL), tok(D_MODEL),
                     _layer((RET_W, D_MODEL), layer), _layer((FOX_W, D_MODEL), layer),
                     _layer((D_MODEL, D_MODEL), layer)]
        in_specs, args = [tok(D_MODEL)] + mix_specs + ffn_specs, (x,) + tuple(mixer) + ffn_args
    else:
        in_specs, args = [tok(D_MODEL)] + ffn_specs, (x,) + ffn_args
    return pl.pallas_call(
        functools.partial(_ffn_kernel, tm=tm, sub_rows=sub_rows, mix=mixer is not None,
                          final_norm=final_norm),
        grid=(t // tm,),
        in_specs=in_specs,
        out_specs=tok(D_MODEL),
        out_shape=jax.ShapeDtypeStruct((t, D_MODEL), F32),
        scratch_shapes=[pltpu.VMEM((tm, D_MODEL), F32)],
        compiler_params=_params("arbitrary"),
        name="mix_ffn" if mixer is not None else "ffn",
    )(*args)


def _proj_kernel(x_ref, g_ref, w_ref, wqvf_ref, brow_ref, cos_ref, sin_ref, rn_ref,
                 ret_ref, gr_ref, gf_ref, fqt_ref, fvt_ref, fka_ref, qaug_ref,
                 carry, rq_ref, rk_ref, rv_ref, rg_ref, state_ref, decay_ref, xi_ref, zeta_ref):
    @pl.when((pl.program_id(0) == 0) & (pl.program_id(1) == 0))
    def _():
        _retention_tables(decay_ref, xi_ref, zeta_ref)

    @pl.when(pl.program_id(1) == 0)
    def _():
        carry[...] = jnp.zeros_like(carry)
        state_ref[...] = jnp.zeros_like(state_ref)

    for sub in range(PROJ_TM // PROJ_SUB):
        _proj_rows(slice(sub * PROJ_SUB, (sub + 1) * PROJ_SUB),
                   x_ref, g_ref, w_ref, wqvf_ref, brow_ref, cos_ref, sin_ref,
                   rq_ref, rk_ref, rv_ref, rg_ref, gr_ref, gf_ref,
                   fqt_ref, fvt_ref, fka_ref, qaug_ref, carry)
        _retention_rows(sub * PROJ_SUB, PROJ_SUB, rq_ref, rk_ref, rv_ref, rg_ref, rn_ref, ret_ref,
                        state_ref, decay_ref, xi_ref, zeta_ref)


def _proj_rows(rows, x_ref, g_ref, w_ref, wqvf_ref, brow_ref, cos_ref, sin_ref,
               rq_ref, rk_ref, rv_ref, rg_ref, gr_ref, gf_ref,
               fqt_ref, fvt_ref, fka_ref, qaug_ref, carry):
    tm = PROJ_SUB
    h = _rms(x_ref[rows, :], g_ref[...]).astype(BF16)

    qvf = _dot_nt(wqvf_ref[...], h)
    fqt_ref[0, :, rows] = (qvf[0:FOX_W] * (FOX_DH ** -0.5 * LOG2E)).astype(BF16)
    fvt_ref[0, :, rows] = qvf[FOX_W:2 * FOX_W].astype(BF16)
    k_t = qvf[2 * FOX_W:3 * FOX_W]

    c = jax.nn.log_sigmoid(qvf[3 * FOX_W:3 * FOX_W + BF16_ROWS] + brow_ref[...])
    lane = lax.broadcasted_iota(jnp.int32, (BF16_ROWS, tm), 1)
    shift = 1
    while shift < tm:
        c = c + jnp.where(lane >= shift, pltpu.roll(c, shift, 1), 0.0)
        shift *= 2
    c = c + carry[:, 0:1]
    carry[...] = jnp.broadcast_to(c[:, tm - 1:tm], carry.shape)

    parts = [part.astype(F32) for part in _split3(c * LOG2E)]
    srow = lax.broadcasted_iota(jnp.int32, (BF16_ROWS, tm), 0)
    zeros = jnp.zeros((BF16_ROWS, tm), F32)
    ones_q = jnp.where((srow >= AUG_PARTS) & (srow < AUG_SPAN), 1.0, zeros)
    ones_k = jnp.where(srow < AUG_PARTS, 1.0, zeros)
    q_blocks, k_rows = [], []
    k_pad = jnp.zeros((LANES - FOX_DH - BF16_ROWS, tm), F32)
    for hd in range(FOX_HEADS):
        qb, kb = ones_q, ones_k
        for idx, part in enumerate(parts):
            src = jnp.broadcast_to(part[hd:hd + 1, :], (BF16_ROWS, tm))
            qb = jnp.where(srow == idx, src, qb)
            kb = jnp.where(srow == AUG_PARTS + idx, -src, kb)
        q_blocks.append(qb)
        k_rows += [k_t[hd * FOX_DH:(hd + 1) * FOX_DH], kb, k_pad]
    qaug_ref[0, :, rows] = jnp.concatenate(q_blocks, axis=0).astype(BF16)
    fka_ref[rows, :] = jnp.concatenate(k_rows, axis=0).T.astype(BF16)

    def cols(off, width):
        return _dot(h, w_ref[:, off:off + width])

    cos = cos_ref[rows, :]
    sin = sin_ref[rows, :]

    def rope(t):
        parts = []
        for hd in range(RET_HEADS):
            th = t[:, hd * RET_DK:(hd + 1) * RET_DK]
            parts.append(th * cos + pltpu.roll(th, RET_DK // 2, 1) * sin)
        return jnp.concatenate(parts, axis=-1)

    rq_ref[rows, :] = rope(cols(0, RET_W)).astype(BF16)
    rk_ref[rows, :] = (rope(cols(RET_W, RET_W)) * (RET_DK ** -0.5)).astype(BF16)
    rv_ref[rows, :] = cols(2 * RET_W, RET_W).astype(BF16)
    rg_ref[rows, :] = cols(3 * RET_W, RET_W).astype(BF16)
    gr_ref[rows, :] = jax.nn.sigmoid(cols(GATE_OFF, D_MODEL)).astype(BF16)
    gf_ref[rows, :] = jax.nn.sigmoid(cols(GATE_OFF + D_MODEL, D_MODEL)).astype(BF16)


def _proj(x, g, w_main, wqvf_t, brow, cos, sin, ret_norm, layer, batch, seq):
    t = x.shape[0]
    tm = PROJ_TM
    ns = seq // tm

    def tok(width):
        return pl.BlockSpec((tm, width), lambda b, s: (b * ns + s, 0))

    def feat(rows):
        return pl.BlockSpec((1, rows, tm), lambda b, s: (b, 0, s))

    pos = pl.BlockSpec((tm, RET_DK), lambda b, s: (s, 0))
    out_shapes = [jax.ShapeDtypeStruct((t, RET_W), BF16)] + [
        jax.ShapeDtypeStruct((t, D_MODEL), BF16)] * 2 + [
        jax.ShapeDtypeStruct((batch, FOX_W, seq), BF16)] * 2 + [
        jax.ShapeDtypeStruct((t, FOX_HEADS * LANES), BF16),
        jax.ShapeDtypeStruct((batch, FOX_HEADS * BF16_ROWS, seq), BF16)]
    out_specs = [tok(RET_W)] + [tok(D_MODEL)] * 2 + [
        feat(FOX_W), feat(FOX_W), tok(FOX_HEADS * LANES), feat(FOX_HEADS * BF16_ROWS)]
    return pl.pallas_call(
        _proj_kernel,
        grid=(batch, ns),
        in_specs=[tok(D_MODEL), _layer((1, D_MODEL), layer), _layer((D_MODEL, MAIN_COLS), layer),
                  _layer((3 * FOX_W + BF16_ROWS, D_MODEL), layer),
                  _layer((BF16_ROWS, 1), layer), pos, pos, _layer((1, RET_W), layer)],
        out_specs=out_specs,
        out_shape=out_shapes,
        scratch_shapes=[pltpu.VMEM((BF16_ROWS, LANES), F32)] + [
            pltpu.VMEM((tm, RET_W), BF16)] * 4 + [
            pltpu.VMEM((RET_HEADS, RET_DK, RET_DV), F32),
            pltpu.VMEM((RET_HEADS, RET_C, RET_C), F32),
            pltpu.VMEM((RET_HEADS, RET_C, RET_DV), F32),
            pltpu.VMEM((RET_HEADS, RET_C, RET_DV), F32)],
        compiler_params=_params("arbitrary", "arbitrary"),
        name="mix_proj",
    )(x, g, w_main, wqvf_t, brow, cos, sin, ret_norm)


def _retention_tables(decay_ref, xi_ref, zeta_ref):
    c = RET_C
    row = lax.broadcasted_iota(jnp.int32, (c, c), 0)
    col = lax.broadcasted_iota(jnp.int32, (c, c), 1)
    diff = (row - col).astype(F32)
    pos = lax.broadcasted_iota(jnp.int32, (c, RET_DV), 0).astype(F32)
    for hd in range(RET_HEADS):
        log_gamma = float(np.log1p(-np.exp2(-5.0 - hd)))
        decay_ref[hd] = jnp.where(diff >= 0, jnp.exp(log_gamma * jnp.maximum(diff, 0.0)), 0.0)
        xi_ref[hd] = jnp.exp(log_gamma * (pos + 1.0))
        zeta_ref[hd] = jnp.exp(log_gamma * (c - 1.0 - pos))


def _retention_rows(first_row, n_rows, q_ref, k_ref, v_ref, rg_ref, rn_ref, o_ref,
                    state_ref, decay_ref, xi_ref, zeta_ref):
    c = RET_C
    for ch in range(n_rows // c):
        rows = slice(first_row + ch * c, first_row + (ch + 1) * c)
        for hd in range(RET_HEADS):
            log_gamma = float(np.log1p(-np.exp2(-5.0 - hd)))
            sl = slice(hd * RET_DK, (hd + 1) * RET_DK)
            q = q_ref[rows, sl]
            k = k_ref[rows, sl]
            v = v_ref[rows, sl]
            scores = _dot_nt(q, k) * decay_ref[hd]
            inner = _dot(scores.astype(BF16), v)
            state = state_ref[hd]
            cross = _dot(q, state.astype(BF16)) * xi_ref[hd]
            vz = (v.astype(F32) * zeta_ref[hd]).astype(BF16)
            state_ref[hd] = float(np.exp(log_gamma * c)) * state + _dot_tn(k, vz)
            ret = inner + cross
            ret = ret * lax.rsqrt(jnp.mean(ret * ret, axis=-1, keepdims=True) + EPS)
            gate = jax.nn.silu(rg_ref[rows, sl].astype(F32))
            o_ref[rows, sl] = (gate * (ret * rn_ref[:, sl])).astype(BF16)


def _fox_kernel(qt_ref, qaug_ref, k_ref, vt_ref, o_ref,
                w_ref, s_ref, s2_ref, c_ref, c2_ref, p_ref, a_ref, m_ref, acc_ref):
    tq, tk = FOX_TQ, FOX_TK
    qi = pl.program_id(2)
    for j in range(FOX_GROUP):
        w_ref[j, 0:FOX_DH, :] = qt_ref[0, j * FOX_DH:(j + 1) * FOX_DH, :]
        w_ref[j, FOX_DH:FOX_DH + BF16_ROWS, :] = qaug_ref[0, j * BF16_ROWS:(j + 1) * BF16_ROWS, :]
        w_ref[j, FOX_DH + BF16_ROWS:LANES, :] = jnp.zeros((LANES - FOX_DH - BF16_ROWS, tq), BF16)
    m_ref[...] = jnp.full_like(m_ref, NEG)
    acc_ref[...] = jnp.zeros_like(acc_ref)
    p_ref[...] = jnp.zeros_like(p_ref)
    a_ref[...] = jnp.ones_like(a_ref)

    def rows_of(kv):
        return pl.ds(pl.multiple_of(kv * tk, tk), tk)

    def score_head(j, kv, dst):
        dst_s, dst_c = dst
        rows = rows_of(kv)
        s = _dot(k_ref[rows, j * LANES:(j + 1) * LANES], w_ref[j])
        dst_s[j] = s
        dst_c[j] = jnp.max(s, axis=0, keepdims=True)

    def softmax_head(j, src, diag):
        src_s, src_c = src
        s = src_s[j]
        if diag is not None:
            r = lax.broadcasted_iota(jnp.int32, (tk, tq), 0)
            cidx = lax.broadcasted_iota(jnp.int32, (tk, tq), 1)
            s = jnp.where(r + diag * tk <= cidx, s, NEG)
            cmax = jnp.max(s, axis=0, keepdims=True)
        else:
            cmax = src_c[j]
        m_old = m_ref[j]
        m_new = jnp.maximum(m_old, cmax)
        alpha = jnp.exp2(m_old - m_new)
        p = jnp.exp2(s - m_new)
        m_ref[j] = m_new
        a_ref[j] = alpha
        p_ref[j] = p.astype(BF16)

    def value_head(j, kv):
        rows = rows_of(kv)
        ones = jnp.ones((BF16_ROWS, tk), BF16)
        vt = jnp.concatenate([vt_ref[0, j * FOX_DH:(j + 1) * FOX_DH, rows], ones], axis=0)
        acc_ref[j] = a_ref[j] * acc_ref[j] + _dot(vt, p_ref[j])

    def score_stage(kv, dst):
        for j in range(FOX_GROUP):
            score_head(j, kv, dst)

    def softmax_stage(src, diag):
        for j in range(FOX_GROUP):
            softmax_head(j, src, diag)

    def value_stage(kv):
        for j in range(FOX_GROUP):
            value_head(j, kv)

    def step(i, src, dst, diag=None):
        prev = jnp.maximum(i - 1, 0)
        for j in range(FOX_GROUP):
            score_head(j, i + 1, dst)
            value_head(j, prev)
            softmax_head(j, src, diag)

    buf_a = (s_ref, c_ref)
    buf_b = (s2_ref, c2_ref)

    def body(h, carry):
        step(2 * h, buf_a, buf_b)
        step(2 * h + 1, buf_b, buf_a)
        return carry

    assert tq == 2 * tk
    first_diag = 2 * qi
    score_stage(0, buf_a)
    lax.fori_loop(0, qi, body, 0)
    step(first_diag, buf_a, buf_b, diag=0)

    @pl.when(qi >= 0)
    def _():
        value_stage(first_diag)
        softmax_stage(buf_b, 1)

    @pl.when(qi >= 0)
    def _():
        value_stage(first_diag + 1)
    out_t = jnp.concatenate(
        [acc_ref[j, 0:FOX_DH, :] / acc_ref[j, FOX_DH:FOX_DH + 1, :] for j in range(FOX_GROUP)], axis=0)
    o_ref[...] = out_t.T.astype(BF16)


def _fox(fqt, qaug, fka, fvt, batch, seq):
    t = fka.shape[0]
    tq, tk = FOX_TQ, FOX_TK
    nq = seq // tq
    gw = FOX_GROUP * FOX_DH
    return pl.pallas_call(
        _fox_kernel,
        grid=(batch, FOX_HEADS // FOX_GROUP, nq),
        in_specs=[
            pl.BlockSpec((1, gw, tq), lambda b, g, i: (b, g, i)),
            pl.BlockSpec((1, FOX_GROUP * BF16_ROWS, tq), lambda b, g, i: (b, g, i)),
            pl.BlockSpec((seq, FOX_GROUP * LANES), lambda b, g, i: (b, g)),
            pl.BlockSpec((1, gw, seq), lambda b, g, i: (b, g, 0)),
        ],
        out_specs=pl.BlockSpec((tq, gw), lambda b, g, i: (b * nq + i, g)),
        out_shape=jax.ShapeDtypeStruct((t, FOX_W), BF16),
        scratch_shapes=[pltpu.VMEM((FOX_GROUP, LANES, tq), BF16),
                        pltpu.VMEM((FOX_GROUP, tk, tq), F32),
                        pltpu.VMEM((FOX_GROUP, tk, tq), F32),
                        pltpu.VMEM((FOX_GROUP, 1, tq), F32),
                        pltpu.VMEM((FOX_GROUP, 1, tq), F32),
                        pltpu.VMEM((FOX_GROUP, tk, tq), BF16),
                        pltpu.VMEM((FOX_GROUP, 1, tq), F32),
                        pltpu.VMEM((FOX_GROUP, 1, tq), F32),
                        pltpu.VMEM((FOX_GROUP, FOX_DH + BF16_ROWS, tq), F32)],
        compiler_params=_params("arbitrary", "arbitrary", "arbitrary"),
        name="fox_attn",
    )(fqt, qaug, fka, fvt)


def _rope_tables(seq):
    d = RET_DK
    inv = jnp.power(ROPE_BASE, -jnp.arange(0, d, 2, dtype=F32) / d)
    ang = jnp.arange(seq, dtype=F32)[:, None] * inv[None, :]
    cos, sin = jnp.cos(ang), jnp.sin(ang)
    return jnp.concatenate([cos, cos], axis=-1), jnp.concatenate([-sin, sin], axis=-1)


def kernel(x, norm_ffn1, w_ffn1_in, w_ffn1_out, norm_mix, w_in, b_forget, ret_norm,
           w_o_ret, w_o_fox, w_out, norm_ffn2, w_ffn2_in, w_ffn2_out, norm_final):
    batch, seq, d = x.shape
    assert d == D_MODEL and seq % PROJ_TM == 0 and seq % FOX_TQ == 0 and PROJ_SUB % RET_C == 0
    assert all((batch * seq) % tile == 0 for tile in (FFN_TM, MIX_FFN_TM))
    xt = x.reshape(batch * seq, d)

    w1i = w_ffn1_in.astype(BF16)
    w1o = w_ffn1_out.astype(BF16)
    w2i = w_ffn2_in.astype(BF16)
    w2o = w_ffn2_out.astype(BF16)
    fq_off = 4 * RET_W
    w_main = jnp.concatenate([w_in[..., :fq_off], w_in[..., FF_OFF + FOX_HEADS:]], axis=-1).astype(BF16)
    wqvf_t = jnp.swapaxes(jnp.concatenate(
        [w_in[..., fq_off:fq_off + FOX_W], w_in[..., fq_off + 2 * FOX_W:FF_OFF],
         w_in[..., fq_off + FOX_W:fq_off + 2 * FOX_W], w_in[..., FF_OFF:FF_OFF + FOX_HEADS],
         jnp.zeros((DEPTH, D_MODEL, BF16_ROWS - FOX_HEADS), w_in.dtype)], axis=-1).astype(BF16), 1, 2)
    brow = jnp.pad(b_forget, ((0, 0), (0, BF16_ROWS - FOX_HEADS)))[:, :, None]
    wor = w_o_ret.astype(BF16)
    wof = w_o_fox.astype(BF16)
    wo = w_out.astype(BF16)
    cos, sin = _rope_tables(seq)
    g_final = norm_final[None, :]
    g_ffn1 = norm_ffn1[:, None, :]
    g_mix = norm_mix[:, None, :]
    g_ffn2 = norm_ffn2[:, None, :]
    g_ret = ret_norm[:, None, :]

    for l in range(DEPTH):
        xt = _ffn(xt, g_ffn1, w1i, w1o, g_final, l, False)
        ret, gr, gf, fqt, fvt, fka, qaug = _proj(
            xt, g_mix, w_main, wqvf_t, brow, cos, sin, g_ret, l, batch, seq)
        fox = _fox(fqt, qaug, fka, fvt, batch, seq)
        xt = _ffn(xt, g_ffn2, w2i, w2o, g_final, l, l == DEPTH - 1,
                  mixer=(ret, fox, gr, gf, wor, wof, wo))
    return xt.reshape(batch, seq, d)
```

```python
import functools
import math

import jax
import jax.numpy as jnp
import numpy as np
from jax import lax
from jax.experimental import pallas as pl
from jax.experimental.pallas import tpu as pltpu

D_MODEL = 1024
DEPTH = 4
RET_HEADS = 4
RET_DK = 128
RET_DV = 128
FOX_HEADS = 8
FOX_DH = 64
D_FF = 2816
ROPE_BASE = 10000.0
EPS = 1e-6

RET_W = RET_HEADS * RET_DK
FOX_W = FOX_HEADS * FOX_DH
FF_OFF = 4 * RET_W + 3 * FOX_W
MAIN_COLS = 4 * RET_W + 2 * D_MODEL
GATE_OFF = 4 * RET_W

LANES = 128
MXU_COLS = 256
BF16_ROWS = 16
VMEM_LIMIT_BYTES = 56 * 1024 * 1024

FFN_TM = 1024
FFN_SUB = 512
MIX_FFN_TM = 512
MIX_FFN_SUB = 512
FFN_CH = MXU_COLS
PROJ_TM = 1024
PROJ_SUB = 512
RET_C = 256
FOX_TQ = 512
FOX_TK = 256
FOX_GROUP = 8
NEG = -1e30
LOG2E = math.log2(math.e)

AUG_PARTS = 3
AUG_SPAN = 2 * AUG_PARTS

BF16 = jnp.bfloat16
F32 = jnp.float32


def _dot(a, b):
    return jnp.dot(a, b, preferred_element_type=F32)


def _dot_nt(a, b):
    return lax.dot_general(a, b, (((1,), (1,)), ((), ())), preferred_element_type=F32)


def _dot_tn(a, b):
    return lax.dot_general(a, b, (((0,), (0,)), ((), ())), preferred_element_type=F32)


def _rms(x, g):
    ms = jnp.mean(x * x, axis=-1, keepdims=True)
    return x * lax.rsqrt(ms + EPS) * g


def _split3(x):
    hi = x.astype(BF16)
    r = x - hi.astype(F32)
    mid = r.astype(BF16)
    lo = (r - mid.astype(F32)).astype(BF16)
    return hi, mid, lo


def _resident(shape):
    nd = len(shape)
    return pl.BlockSpec(shape, lambda *_: (0,) * nd, pipeline_mode=pl.Buffered(1))


def _layer(shape, layer):
    nd = len(shape)
    return pl.BlockSpec((None,) + tuple(shape), lambda *_: (layer,) + (0,) * nd,
                        pipeline_mode=pl.Buffered(1))


def _params(*sem):
    return pltpu.CompilerParams(dimension_semantics=sem, vmem_limit_bytes=VMEM_LIMIT_BYTES)


def _ffn_kernel(*refs, tm, sub_rows, mix, final_norm):
    if mix:
        (x_ref, ret_ref, fox_ref, gr_ref, gf_ref, wor_ref, wof_ref, wmix_ref,
         g_ref, win_ref, wout_ref, gfin_ref, o_ref, acc_ref) = refs
    else:
        x_ref, g_ref, win_ref, wout_ref, gfin_ref, o_ref, acc_ref = refs
    for sub in range(tm // sub_rows):
        rows = slice(sub * sub_rows, (sub + 1) * sub_rows)
        x = x_ref[rows, :]
        if mix:
            y_ret = _dot(ret_ref[rows, :], wor_ref[...])
            y_fox = _dot(fox_ref[rows, :], wof_ref[...])
            merged = gr_ref[rows, :].astype(F32) * y_ret + gf_ref[rows, :].astype(F32) * y_fox
            x = x + _dot(merged.astype(BF16), wmix_ref[...])
        xn = _rms(x, g_ref[...]).astype(BF16)
        for c in range(D_FF // FFN_CH):
            lo = c * FFN_CH
            a = _dot(xn, win_ref[:, lo:lo + FFN_CH])
            b = _dot(xn, win_ref[:, D_FF + lo:D_FF + lo + FFN_CH])
            h = (a * jax.nn.sigmoid(a) * b).astype(BF16)
            y = _dot(h, wout_ref[lo:lo + FFN_CH, :])
            if c == 0:
                acc_ref[rows, :] = y
            else:
                acc_ref[rows, :] += y
        out = x + 0.5 * acc_ref[rows, :]
        if final_norm:
            out = _rms(out, gfin_ref[...])
        o_ref[rows, :] = out


def _ffn(x, g, w_in, w_out, g_final, layer, final_norm, mixer=None):
    t = x.shape[0]
    tm, sub_rows = (MIX_FFN_TM, MIX_FFN_SUB) if mixer is not None else (FFN_TM, FFN_SUB)

    def tok(width):
        return pl.BlockSpec((tm, width), lambda i: (i, 0))

    ffn_specs = [_layer((1, D_MODEL), layer), _layer((D_MODEL, 2 * D_FF), layer),
                 _layer((D_FF, D_MODEL), layer), _resident((1, D_MODEL))]
    ffn_args = (g, w_in, w_out, g_final)
    if mixer is not None:
        mix_specs = [tok(RET_W), tok(FOX_W), tok(D_MODEL), tok(D_MODEL),
                     _layer((RET_W, D_MODEL), layer), _layer((FOX_W, D_MODEL), layer),
                     _layer((D_MODEL, D_MODEL), layer)]
        in_specs, args = [tok(D_MODEL)] + mix_specs + ffn_specs, (x,) + tuple(mixer) + ffn_args
    else:
        in_specs, args = [tok(D_MODEL)] + ffn_specs, (x,) + ffn_args
    return pl.pallas_call(
        functools.partial(_ffn_kernel, tm=tm, sub_rows=sub_rows, mix=mixer is not None,
                          final_norm=final_norm),
        grid=(t // tm,),
        in_specs=in_specs,
        out_specs=tok(D_MODEL),
        out_shape=jax.ShapeDtypeStruct((t, D_MODEL), F32),
        scratch_shapes=[pltpu.VMEM((tm, D_MODEL), F32)],
        compiler_params=_params("arbitrary"),
        name="mix_ffn" if mixer is not None else "ffn",
    )(*args)


def _proj_kernel(x_ref, g_ref, w_ref, wqvf_ref, brow_ref, cos_ref, sin_ref, rn_ref,
                 ret_ref, gr_ref, gf_ref, fqt_ref, fvt_ref, fka_ref, qaug_ref,
                 carry, rq_ref, rk_ref, rv_ref, rg_ref, state_ref, decay_ref, xi_ref, zeta_ref):
    @pl.when((pl.program_id(0) == 0) & (pl.program_id(1) == 0))
    def _():
        _retention_tables(decay_ref, xi_ref, zeta_ref)

    @pl.when(pl.program_id(1) == 0)
    def _():
        carry[...] = jnp.zeros_like(carry)
        state_ref[...] = jnp.zeros_like(state_ref)

    for sub in range(PROJ_TM // PROJ_SUB):
        _proj_rows(slice(sub * PROJ_SUB, (sub + 1) * PROJ_SUB),
                   x_ref, g_ref, w_ref, wqvf_ref, brow_ref, cos_ref, sin_ref,
                   rq_ref, rk_ref, rv_ref, rg_ref, gr_ref, gf_ref,
                   fqt_ref, fvt_ref, fka_ref, qaug_ref, carry)
        _retention_rows(sub * PROJ_SUB, PROJ_SUB, rq_ref, rk_ref, rv_ref, rg_ref, rn_ref, ret_ref,
                        state_ref, decay_ref, xi_ref, zeta_ref)


def _proj_rows(rows, x_ref, g_ref, w_ref, wqvf_ref, brow_ref, cos_ref, sin_ref,
               rq_ref, rk_ref, rv_ref, rg_ref, gr_ref, gf_ref,
               fqt_ref, fvt_ref, fka_ref, qaug_ref, carry):
    tm = PROJ_SUB
    h = _rms(x_ref[rows, :], g_ref[...]).astype(BF16)

    qvf = _dot_nt(wqvf_ref[...], h)
    fqt_ref[0, :, rows] = (qvf[0:FOX_W] * (FOX_DH ** -0.5 * LOG2E)).astype(BF16)
    fvt_ref[0, :, rows] = qvf[FOX_W:2 * FOX_W].astype(BF16)
    k_t = qvf[2 * FOX_W:3 * FOX_W]

    c = jax.nn.log_sigmoid(qvf[3 * FOX_W:3 * FOX_W + BF16_ROWS] + brow_ref[...])
    lane = lax.broadcasted_iota(jnp.int32, (BF16_ROWS, tm), 1)
    shift = 1
    while shift < tm:
        c = c + jnp.where(lane >= shift, pltpu.roll(c, shift, 1), 0.0)
        shift *= 2
    c = c + carry[:, 0:1]
    carry[...] = jnp.broadcast_to(c[:, tm - 1:tm], carry.shape)

    parts = [part.astype(F32) for part in _split3(c * LOG2E)]
    srow = lax.broadcasted_iota(jnp.int32, (BF16_ROWS, tm), 0)
    zeros = jnp.zeros((BF16_ROWS, tm), F32)
    ones_q = jnp.where((srow >= AUG_PARTS) & (srow < AUG_SPAN), 1.0, zeros)
    ones_k = jnp.where(srow < AUG_PARTS, 1.0, zeros)
    q_blocks, k_rows = [], []
    k_pad = jnp.zeros((LANES - FOX_DH - BF16_ROWS, tm), F32)
    for hd in range(FOX_HEADS):
        qb, kb = ones_q, ones_k
        for idx, part in enumerate(parts):
            src = jnp.broadcast_to(part[hd:hd + 1, :], (BF16_ROWS, tm))
            qb = jnp.where(srow == idx, src, qb)
            kb = jnp.where(srow == AUG_PARTS + idx, -src, kb)
        q_blocks.append(qb)
        k_rows += [k_t[hd * FOX_DH:(hd + 1) * FOX_DH], kb, k_pad]
    qaug_ref[0, :, rows] = jnp.concatenate(q_blocks, axis=0).astype(BF16)
    fka_ref[rows, :] = jnp.concatenate(k_rows, axis=0).T.astype(BF16)

    def cols(off, width):
        return _dot(h, w_ref[:, off:off + width])

    cos = cos_ref[rows, :]
    sin = sin_ref[rows, :]

    def rope(t):
        parts = []
        for hd in range(RET_HEADS):
            th = t[:, hd * RET_DK:(hd + 1) * RET_DK]
            parts.append(th * cos + pltpu.roll(th, RET_DK // 2, 1) * sin)
        return jnp.concatenate(parts, axis=-1)

    rq_ref[rows, :] = rope(cols(0, RET_W)).astype(BF16)
    rk_ref[rows, :] = (rope(cols(RET_W, RET_W)) * (RET_DK ** -0.5)).astype(BF16)
    rv_ref[rows, :] = cols(2 * RET_W, RET_W).astype(BF16)
    rg_ref[rows, :] = cols(3 * RET_W, RET_W).astype(BF16)
    gr_ref[rows, :] = jax.nn.sigmoid(cols(GATE_OFF, D_MODEL)).astype(BF16)
    gf_ref[rows, :] = jax.nn.sigmoid(cols(GATE_OFF + D_MODEL, D_MODEL)).astype(BF16)


def _proj(x, g, w_main, wqvf_t, brow, cos, sin, ret_norm, layer, batch, seq):
    t = x.shape[0]
    tm = PROJ_TM
    ns = seq // tm

    def tok(width):
        return pl.BlockSpec((tm, width), lambda b, s: (b * ns + s, 0))

    def feat(rows):
        return pl.BlockSpec((1, rows, tm), lambda b, s: (b, 0, s))

    pos = pl.BlockSpec((tm, RET_DK), lambda b, s: (s, 0))
    out_shapes = [jax.ShapeDtypeStruct((t, RET_W), BF16)] + [
        jax.ShapeDtypeStruct((t, D_MODEL), BF16)] * 2 + [
        jax.ShapeDtypeStruct((batch, FOX_W, seq), BF16)] * 2 + [
        jax.ShapeDtypeStruct((t, FOX_HEADS * LANES), BF16),
        jax.ShapeDtypeStruct((batch, FOX_HEADS * BF16_ROWS, seq), BF16)]
    out_specs = [tok(RET_W)] + [tok(D_MODEL)] * 2 + [
        feat(FOX_W), feat(FOX_W), tok(FOX_HEADS * LANES), feat(FOX_HEADS * BF16_ROWS)]
    return pl.pallas_call(
        _proj_kernel,
        grid=(batch, ns),
        in_specs=[tok(D_MODEL), _layer((1, D_MODEL), layer), _layer((D_MODEL, MAIN_COLS), layer),
                  _layer((3 * FOX_W + BF16_ROWS, D_MODEL), layer),
                  _layer((BF16_ROWS, 1), layer), pos, pos, _layer((1, RET_W), layer)],
        out_specs=out_specs,
        out_shape=out_shapes,
        scratch_shapes=[pltpu.VMEM((BF16_ROWS, LANES), F32)] + [
            pltpu.VMEM((tm, RET_W), BF16)] * 4 + [
            pltpu.VMEM((RET_HEADS, RET_DK, RET_DV), F32),
            pltpu.VMEM((RET_HEADS, RET_C, RET_C), F32),
            pltpu.VMEM((RET_HEADS, RET_C, RET_DV), F32),
            pltpu.VMEM((RET_HEADS, RET_C, RET_DV), F32)],
        compiler_params=_params("arbitrary", "arbitrary"),
        name="mix_proj",
    )(x, g, w_main, wqvf_t, brow, cos, sin, ret_norm)


def _retention_tables(decay_ref, xi_ref, zeta_ref):
    c = RET_C
    row = lax.broadcasted_iota(jnp.int32, (c, c), 0)
    col = lax.broadcasted_iota(jnp.int32, (c, c), 1)
    diff = (row - col).astype(F32)
    pos = lax.broadcasted_iota(jnp.int32, (c, RET_DV), 0).astype(F32)
    for hd in range(RET_HEADS):
        log_gamma = float(np.log1p(-np.exp2(-5.0 - hd)))
        decay_ref[hd] = jnp.where(diff >= 0, jnp.exp(log_gamma * jnp.maximum(diff, 0.0)), 0.0)
        xi_ref[hd] = jnp.exp(log_gamma * (pos + 1.0))
        zeta_ref[hd] = jnp.exp(log_gamma * (c - 1.0 - pos))


def _retention_rows(first_row, n_rows, q_ref, k_ref, v_ref, rg_ref, rn_ref, o_ref,
                    state_ref, decay_ref, xi_ref, zeta_ref):
    c = RET_C
    for ch in range(n_rows // c):
        rows = slice(first_row + ch * c, first_row + (ch + 1) * c)
        for hd in range(RET_HEADS):
            log_gamma = float(np.log1p(-np.exp2(-5.0 - hd)))
            sl = slice(hd * RET_DK, (hd + 1) * RET_DK)
            q = q_ref[rows, sl]
            k = k_ref[rows, sl]
            v = v_ref[rows, sl]
            scores = _dot_nt(q, k) * decay_ref[hd]
            inner = _dot(scores.astype(BF16), v)
            state = state_ref[hd]
            cross = _dot(q, state.astype(BF16)) * xi_ref[hd]
            vz = (v.astype(F32) * zeta_ref[hd]).astype(BF16)
            state_ref[hd] = float(np.exp(log_gamma * c)) * state + _dot_tn(k, vz)
            ret = inner + cross
            ret = ret * lax.rsqrt(jnp.mean(ret * ret, axis=-1, keepdims=True) + EPS)
            gate = jax.nn.silu(rg_ref[rows, sl].astype(F32))
            o_ref[rows, sl] = (gate * (ret * rn_ref[:, sl])).astype(BF16)


def _fox_kernel(qt_ref, qaug_ref, k_ref, vt_ref, o_ref,
                w_ref, s_ref, s2_ref, c_ref, c2_ref, p_ref, a_ref, m_ref, acc_ref):
    tq, tk = FOX_TQ, FOX_TK
    qi = pl.program_id(2)
    for j in range(FOX_GROUP):
        w_ref[j, 0:FOX_DH, :] = qt_ref[0, j * FOX_DH:(j + 1) * FOX_DH, :]
        w_ref[j, FOX_DH:FOX_DH + BF16_ROWS, :] = qaug_ref[0, j * BF16_ROWS:(j + 1) * BF16_ROWS, :]
        w_ref[j, FOX_DH + BF16_ROWS:LANES, :] = jnp.zeros((LANES - FOX_DH - BF16_ROWS, tq), BF16)
    m_ref[...] = jnp.full_like(m_ref, NEG)
    acc_ref[...] = jnp.zeros_like(acc_ref)
    p_ref[...] = jnp.zeros_like(p_ref)
    a_ref[...] = jnp.ones_like(a_ref)

    def rows_of(kv):
        return pl.ds(pl.multiple_of(kv * tk, tk), tk)

    def score_head(j, kv, dst):
        dst_s, dst_c = dst
        rows = rows_of(kv)
        s = _dot(k_ref[rows, j * LANES:(j + 1) * LANES], w_ref[j])
        dst_s[j] = s
        dst_c[j] = jnp.max(s, axis=0, keepdims=True)

    def softmax_head(j, src, diag):
        src_s, src_c = src
        s = src_s[j]
        if diag is not None:
            r = lax.broadcasted_iota(jnp.int32, (tk, tq), 0)
            cidx = lax.broadcasted_iota(jnp.int32, (tk, tq), 1)
            s = jnp.where(r + diag * tk <= cidx, s, NEG)
            cmax = jnp.max(s, axis=0, keepdims=True)
        else:
            cmax = src_c[j]
        m_old = m_ref[j]
        m_new = jnp.maximum(m_old, cmax)
        alpha = jnp.exp2(m_old - m_new)
        p = jnp.exp2(s - m_new)
        m_ref[j] = m_new
        a_ref[j] = alpha
        p_ref[j] = p.astype(BF16)

    def value_head(j, kv):
        rows = rows_of(kv)
        ones = jnp.ones((BF16_ROWS, tk), BF16)
        vt = jnp.concatenate([vt_ref[0, j * FOX_DH:(j + 1) * FOX_DH, rows], ones], axis=0)
        acc_ref[j] = a_ref[j] * acc_ref[j] + _dot(vt, p_ref[j])

    def score_stage(kv, dst):
        for j in range(FOX_GROUP):
            score_head(j, kv, dst)

    def softmax_stage(src, diag):
        for j in range(FOX_GROUP):
            softmax_head(j, src, diag)

    def value_stage(kv):
        for j in range(FOX_GROUP):
            value_head(j, kv)

    def step(i, src, dst, diag=None):
        prev = jnp.maximum(i - 1, 0)
        for j in range(FOX_GROUP):
            score_head(j, i + 1, dst)
            value_head(j, prev)
            softmax_head(j, src, diag)

    buf_a = (s_ref, c_ref)
    buf_b = (s2_ref, c2_ref)

    def body(h, carry):
        step(2 * h, buf_a, buf_b)
        step(2 * h + 1, buf_b, buf_a)
        return carry

    assert tq == 2 * tk
    first_diag = 2 * qi
    score_stage(0, buf_a)

    def body_pair(t, carry):
        body(2 * t, carry)
        body(2 * t + 1, carry)
        return carry

    lax.fori_loop(0, qi // 2, body_pair, 0)

    @pl.when(qi % 2 == 1)
    def _():
        body(qi - 1, 0)

    step(first_diag, buf_a, buf_b, diag=0)

    @pl.when(qi >= 0)
    def _():
        value_stage(first_diag)
        softmax_stage(buf_b, 1)

    @pl.when(qi >= 0)
    def _():
        value_stage(first_diag + 1)
    out_t = jnp.concatenate(
        [acc_ref[j, 0:FOX_DH, :] / acc_ref[j, FOX_DH:FOX_DH + 1, :] for j in range(FOX_GROUP)], axis=0)
    o_ref[...] = out_t.T.astype(BF16)


def _fox(fqt, qaug, fka, fvt, batch, seq):
    t = fka.shape[0]
    tq, tk = FOX_TQ, FOX_TK
    nq = seq // tq
    gw = FOX_GROUP * FOX_DH
    return pl.pallas_call(
        _fox_kernel,
        grid=(batch, FOX_HEADS // FOX_GROUP, nq),
        in_specs=[
            pl.BlockSpec((1, gw, tq), lambda b, g, i: (b, g, i)),
            pl.BlockSpec((1, FOX_GROUP * BF16_ROWS, tq), lambda b, g, i: (b, g, i)),
            pl.BlockSpec((seq, FOX_GROUP * LANES), lambda b, g, i: (b, g)),
            pl.BlockSpec((1, gw, seq), lambda b, g, i: (b, g, 0)),
        ],
        out_specs=pl.BlockSpec((tq, gw), lambda b, g, i: (b * nq + i, g)),
        out_shape=jax.ShapeDtypeStruct((t, FOX_W), BF16),
        scratch_shapes=[pltpu.VMEM((FOX_GROUP, LANES, tq), BF16),
                        pltpu.VMEM((FOX_GROUP, tk, tq), F32),
                        pltpu.VMEM((FOX_GROUP, tk, tq), F32),
                        pltpu.VMEM((FOX_GROUP, 1, tq), F32),
                        pltpu.VMEM((FOX_GROUP, 1, tq), F32),
                        pltpu.VMEM((FOX_GROUP, tk, tq), BF16),
                        pltpu.VMEM((FOX_GROUP, 1, tq), F32),
                        pltpu.VMEM((FOX_GROUP, 1, tq), F32),
                        pltpu.VMEM((FOX_GROUP, FOX_DH + BF16_ROWS, tq), F32)],
        compiler_params=_params("arbitrary", "arbitrary", "arbitrary"),
        name="fox_attn",
    )(fqt, qaug, fka, fvt)


def _rope_tables(seq):
    d = RET_DK
    inv = jnp.power(ROPE_BASE, -jnp.arange(0, d, 2, dtype=F32) / d)
    ang = jnp.arange(seq, dtype=F32)[:, None] * inv[None, :]
    cos, sin = jnp.cos(ang), jnp.sin(ang)
    return jnp.concatenate([cos, cos], axis=-1), jnp.concatenate([-sin, sin], axis=-1)


def kernel(x, norm_ffn1, w_ffn1_in, w_ffn1_out, norm_mix, w_in, b_forget, ret_norm,
           w_o_ret, w_o_fox, w_out, norm_ffn2, w_ffn2_in, w_ffn2_out, norm_final):
    batch, seq, d = x.shape
    assert d == D_MODEL and seq % PROJ_TM == 0 and seq % FOX_TQ == 0 and PROJ_SUB % RET_C == 0
    assert all((batch * seq) % tile == 0 for tile in (FFN_TM, MIX_FFN_TM))
    xt = x.reshape(batch * seq, d)

    w1i = w_ffn1_in.astype(BF16)
    w1o = w_ffn1_out.astype(BF16)
    w2i = w_ffn2_in.astype(BF16)
    w2o = w_ffn2_out.astype(BF16)
    fq_off = 4 * RET_W
    w_main = jnp.concatenate([w_in[..., :fq_off], w_in[..., FF_OFF + FOX_HEADS:]], axis=-1).astype(BF16)
    wqvf_t = jnp.swapaxes(jnp.concatenate(
        [w_in[..., fq_off:fq_off + FOX_W], w_in[..., fq_off + 2 * FOX_W:FF_OFF],
         w_in[..., fq_off + FOX_W:fq_off + 2 * FOX_W], w_in[..., FF_OFF:FF_OFF + FOX_HEADS],
         jnp.zeros((DEPTH, D_MODEL, BF16_ROWS - FOX_HEADS), w_in.dtype)], axis=-1).astype(BF16), 1, 2)
    brow = jnp.pad(b_forget, ((0, 0), (0, BF16_ROWS - FOX_HEADS)))[:, :, None]
    wor = w_o_ret.astype(BF16)
    wof = w_o_fox.astype(BF16)
    wo = w_out.astype(BF16)
    cos, sin = _rope_tables(seq)
    g_final = norm_final[None, :]
    g_ffn1 = norm_ffn1[:, None, :]
    g_mix = norm_mix[:, None, :]
    g_ffn2 = norm_ffn2[:, None, :]
    g_ret = ret_norm[:, None, :]

    for l in range(DEPTH):
        xt = _ffn(xt, g_ffn1, w1i, w1o, g_final, l, False)
        ret, gr, gf, fqt, fvt, fka, qaug = _proj(
            xt, g_mix, w_main, wqvf_t, brow, cos, sin, g_ret, l, batch, seq)
        fox = _fox(fqt, qaug, fka, fvt, batch, seq)
        xt = _ffn(xt, g_ffn2, w2i, w2o, g_final, l, l == DEPTH - 1,
                  mixer=(ret, fox, gr, gf, wor, wof, wo))
    return xt.reshape(batch, seq, d)
```

```python
import functools
import math

import jax
import jax.numpy as jnp
import numpy as np
from jax import lax
from jax.experimental import pallas as pl
from jax.experimental.pallas import tpu as pltpu

D_MODEL = 1024
DEPTH = 4
RET_HEADS = 4
RET_DK = 128
RET_DV = 128
FOX_HEADS = 8
FOX_DH = 64
D_FF = 2816
ROPE_BASE = 10000.0
EPS = 1e-6

RET_W = RET_HEADS * RET_DK
FOX_W = FOX_HEADS * FOX_DH
FF_OFF = 4 * RET_W + 3 * FOX_W
MAIN_COLS = 4 * RET_W + 2 * D_MODEL
GATE_OFF = 4 * RET_W
WT_COLS = 3 * FOX_W + 128

LANES = 128
MXU_COLS = 256
BF16_ROWS = 16
VMEM_LIMIT_BYTES = 56 * 1024 * 1024

FFN_TM = 1024
FFN_SUB = 512
MIX_FFN_TM = 512
MIX_FFN_SUB = 512
FFN_CH = MXU_COLS
PROJ_TM = 1024
PROJ_SUB = 512
RET_C = 256
FOX_TQ = 512
FOX_TK = 256
FOX_GROUP = 8
NEG = -1e30
LOG2E = math.log2(math.e)

AUG_PARTS = 3
AUG_SPAN = 2 * AUG_PARTS

BF16 = jnp.bfloat16
F32 = jnp.float32


def _dot(a, b):
    return jnp.dot(a, b, preferred_element_type=F32)


def _dot_nt(a, b):
    return lax.dot_general(a, b, (((1,), (1,)), ((), ())), preferred_element_type=F32)


def _dot_tn(a, b):
    return lax.dot_general(a, b, (((0,), (0,)), ((), ())), preferred_element_type=F32)


def _rms(x, g):
    ms = jnp.mean(x * x, axis=-1, keepdims=True)
    return x * lax.rsqrt(ms + EPS) * g


def _split3(x):
    hi = x.astype(BF16)
    r = x - hi.astype(F32)
    mid = r.astype(BF16)
    lo = (r - mid.astype(F32)).astype(BF16)
    return hi, mid, lo


def _resident(shape):
    nd = len(shape)
    return pl.BlockSpec(shape, lambda *_: (0,) * nd, pipeline_mode=pl.Buffered(1))


def _layer(shape, layer):
    nd = len(shape)
    return pl.BlockSpec((None,) + tuple(shape), lambda *_: (layer,) + (0,) * nd,
                        pipeline_mode=pl.Buffered(1))


def _params(*sem):
    return pltpu.CompilerParams(dimension_semantics=sem, vmem_limit_bytes=VMEM_LIMIT_BYTES)


def _ffn_kernel(*refs, tm, sub_rows, mix, final_norm):
    if mix:
        (x_ref, ret_ref, fox_ref, gr_ref, gf_ref, wor_ref, wof_ref, wmix_ref,
         g_ref, win_ref, wout_ref, gfin_ref, o_ref, acc_ref) = refs
    else:
        x_ref, g_ref, win_ref, wout_ref, gfin_ref, o_ref, acc_ref = refs
    for sub in range(tm // sub_rows):
        rows = slice(sub * sub_rows, (sub + 1) * sub_rows)
        x = x_ref[rows, :]
        if mix:
            y_ret = _dot(ret_ref[rows, :], wor_ref[...])
            y_fox = _dot(fox_ref[rows, :], wof_ref[...])
            merged = gr_ref[rows, :].astype(F32) * y_ret + gf_ref[rows, :].astype(F32) * y_fox
            x = x + _dot(merged.astype(BF16), wmix_ref[...])
        xn = _rms(x, g_ref[...]).astype(BF16)
        for c in range(D_FF // FFN_CH):
            lo = c * FFN_CH
            a = _dot(xn, win_ref[:, lo:lo + FFN_CH])
            b = _dot(xn, win_ref[:, D_FF + lo:D_FF + lo + FFN_CH])
            h = (a * jax.nn.sigmoid(a) * b).astype(BF16)
            y = _dot(h, wout_ref[lo:lo + FFN_CH, :])
            if c == 0:
                acc_ref[rows, :] = y
            else:
                acc_ref[rows, :] += y
        out = x + 0.5 * acc_ref[rows, :]
        if final_norm:
            out = _rms(out, gfin_ref[...])
        o_ref[rows, :] = out


def _ffn(x, g, w_in, w_out, g_final, layer, final_norm, mixer=None):
    t = x.shape[0]
    tm, sub_rows = (MIX_FFN_TM, MIX_FFN_SUB) if mixer is not None else (FFN_TM, FFN_SUB)

    def tok(width):
        return pl.BlockSpec((tm, width), lambda i: (i, 0))

    ffn_specs = [_layer((1, D_MODEL), layer), _layer((D_MODEL, 2 * D_FF), layer),
                 _layer((D_FF, D_MODEL), layer), _resident((1, D_MODEL))]
    ffn_args = (g, w_in, w_out, g_final)
    if mixer is not None:
        mix_specs = [tok(RET_W), tok(FOX_W), tok(D_MODEL), tok(D_MODEL),
                     _layer((RET_W, D_MODEL), layer), _layer((FOX_W, D_MODEL), layer),
                     _layer((D_MODEL, D_MODEL), layer)]
        in_specs, args = [tok(D_MODEL)] + mix_specs + ffn_specs, (x,) + tuple(mixer) + ffn_args
    else:
        in_specs, args = [tok(D_MODEL)] + ffn_specs, (x,) + ffn_args
    return pl.pallas_call(
        functools.partial(_ffn_kernel, tm=tm, sub_rows=sub_rows, mix=mixer is not None,
                          final_norm=final_norm),
        grid=(t // tm,),
        in_specs=in_specs,
        out_specs=tok(D_MODEL),
        out_shape=jax.ShapeDtypeStruct((t, D_MODEL), F32),
        scratch_shapes=[pltpu.VMEM((tm, D_MODEL), F32)],
        compiler_params=_params("arbitrary"),
        name="mix_ffn" if mixer is not None else "ffn",
    )(*args)


def _proj_kernel(x_ref, g_ref, w_ref, wqvf_ref, brow_ref, cos_ref, sin_ref, rn_ref,
                 ret_ref, gr_ref, gf_ref, fqt_ref, fvt_ref, fka_ref, qaug_ref,
                 carry, rq_ref, rk_ref, rv_ref, rg_ref, state_ref, decay_ref, xi_ref, zeta_ref):
    @pl.when((pl.program_id(0) == 0) & (pl.program_id(1) == 0))
    def _():
        _retention_tables(decay_ref, xi_ref, zeta_ref)

    @pl.when(pl.program_id(1) == 0)
    def _():
        carry[...] = jnp.zeros_like(carry)
        state_ref[...] = jnp.zeros_like(state_ref)

    for sub in range(PROJ_TM // PROJ_SUB):
        _proj_rows(slice(sub * PROJ_SUB, (sub + 1) * PROJ_SUB),
                   x_ref, g_ref, w_ref, wqvf_ref, brow_ref, cos_ref, sin_ref,
                   rq_ref, rk_ref, rv_ref, rg_ref, gr_ref, gf_ref,
                   fqt_ref, fvt_ref, fka_ref, qaug_ref, carry)
        _retention_rows(sub * PROJ_SUB, PROJ_SUB, rq_ref, rk_ref, rv_ref, rg_ref, rn_ref, ret_ref,
                        state_ref, decay_ref, xi_ref, zeta_ref)


def _proj_rows(rows, x_ref, g_ref, w_ref, wqvf_ref, brow_ref, cos_ref, sin_ref,
               rq_ref, rk_ref, rv_ref, rg_ref, gr_ref, gf_ref,
               fqt_ref, fvt_ref, fka_ref, qaug_ref, carry):
    tm = PROJ_SUB
    h = _rms(x_ref[rows, :], g_ref[...]).astype(BF16)

    qvf = _dot_nt(wqvf_ref[0:3 * FOX_W + BF16_ROWS, :], h)
    fqt_ref[0, :, rows] = (qvf[0:FOX_W] * (FOX_DH ** -0.5 * LOG2E)).astype(BF16)
    k_t = qvf[FOX_W:2 * FOX_W]
    fvt_ref[0, :, rows] = qvf[2 * FOX_W:3 * FOX_W].astype(BF16)

    c = jax.nn.log_sigmoid(qvf[3 * FOX_W:3 * FOX_W + BF16_ROWS] + brow_ref[...])
    lane = lax.broadcasted_iota(jnp.int32, (BF16_ROWS, tm), 1)
    shift = 1
    while shift < tm:
        c = c + jnp.where(lane >= shift, pltpu.roll(c, shift, 1), 0.0)
        shift *= 2
    c = c + carry[:, 0:1]
    carry[...] = jnp.broadcast_to(c[:, tm - 1:tm], carry.shape)

    parts = [part.astype(F32) for part in _split3(c * LOG2E)]
    srow = lax.broadcasted_iota(jnp.int32, (BF16_ROWS, tm), 0)
    zeros = jnp.zeros((BF16_ROWS, tm), F32)
    ones_q = jnp.where((srow >= AUG_PARTS) & (srow < AUG_SPAN), 1.0, zeros)
    ones_k = jnp.where(srow < AUG_PARTS, 1.0, zeros)
    q_blocks, k_rows = [], []
    k_pad = jnp.zeros((LANES - FOX_DH - BF16_ROWS, tm), F32)
    for hd in range(FOX_HEADS):
        qb, kb = ones_q, ones_k
        for idx, part in enumerate(parts):
            src = jnp.broadcast_to(part[hd:hd + 1, :], (BF16_ROWS, tm))
            qb = jnp.where(srow == idx, src, qb)
            kb = jnp.where(srow == AUG_PARTS + idx, -src, kb)
        q_blocks.append(qb)
        k_rows += [k_t[hd * FOX_DH:(hd + 1) * FOX_DH], kb, k_pad]
    qaug_ref[0, :, rows] = jnp.concatenate(q_blocks, axis=0).astype(BF16)
    fka_ref[rows, :] = jnp.concatenate(k_rows, axis=0).T.astype(BF16)

    def cols(off, width):
        return _dot(h, w_ref[:, off:off + width])

    cos = cos_ref[rows, :]
    sin = sin_ref[rows, :]

    def rope(t):
        parts = []
        for hd in range(RET_HEADS):
            th = t[:, hd * RET_DK:(hd + 1) * RET_DK]
            parts.append(th * cos + pltpu.roll(th, RET_DK // 2, 1) * sin)
        return jnp.concatenate(parts, axis=-1)

    rq_ref[rows, :] = rope(cols(0, RET_W)).astype(BF16)
    rk_ref[rows, :] = (rope(cols(RET_W, RET_W)) * (RET_DK ** -0.5)).astype(BF16)
    rv_ref[rows, :] = cols(2 * RET_W, RET_W).astype(BF16)
    rg_ref[rows, :] = cols(3 * RET_W, RET_W).astype(BF16)
    gr_ref[rows, :] = jax.nn.sigmoid(cols(GATE_OFF, D_MODEL)).astype(BF16)
    gf_ref[rows, :] = jax.nn.sigmoid(cols(GATE_OFF + D_MODEL, D_MODEL)).astype(BF16)


def _proj(x, g, w_main, wqvf_t, brow, cos, sin, ret_norm, layer, batch, seq):
    t = x.shape[0]
    tm = PROJ_TM
    ns = seq // tm

    def tok(width):
        return pl.BlockSpec((tm, width), lambda b, s: (b * ns + s, 0))

    def feat(rows):
        return pl.BlockSpec((1, rows, tm), lambda b, s: (b, 0, s))

    pos = pl.BlockSpec((tm, RET_DK), lambda b, s: (s, 0))
    out_shapes = [jax.ShapeDtypeStruct((t, RET_W), BF16)] + [
        jax.ShapeDtypeStruct((t, D_MODEL), BF16)] * 2 + [
        jax.ShapeDtypeStruct((batch, FOX_W, seq), BF16)] * 2 + [
        jax.ShapeDtypeStruct((t, FOX_HEADS * LANES), BF16),
        jax.ShapeDtypeStruct((batch, FOX_HEADS * BF16_ROWS, seq), BF16)]
    out_specs = [tok(RET_W)] + [tok(D_MODEL)] * 2 + [
        feat(FOX_W), feat(FOX_W), tok(FOX_HEADS * LANES), feat(FOX_HEADS * BF16_ROWS)]
    return pl.pallas_call(
        _proj_kernel,
        grid=(batch, ns),
        in_specs=[tok(D_MODEL), _layer((1, D_MODEL), layer), _layer((D_MODEL, MAIN_COLS), layer),
                  _layer((WT_COLS, D_MODEL), layer),
                  _layer((BF16_ROWS, 1), layer), pos, pos, _layer((1, RET_W), layer)],
        out_specs=out_specs,
        out_shape=out_shapes,
        scratch_shapes=[pltpu.VMEM((BF16_ROWS, LANES), F32)] + [
            pltpu.VMEM((tm, RET_W), BF16)] * 4 + [
            pltpu.VMEM((RET_HEADS, RET_DK, RET_DV), F32),
            pltpu.VMEM((RET_HEADS, RET_C, RET_C), F32),
            pltpu.VMEM((RET_HEADS, RET_C, RET_DV), F32),
            pltpu.VMEM((RET_HEADS, RET_C, RET_DV), F32)],
        compiler_params=_params("arbitrary", "arbitrary"),
        name="mix_proj",
    )(x, g, w_main, wqvf_t, brow, cos, sin, ret_norm)


def _retention_tables(decay_ref, xi_ref, zeta_ref):
    c = RET_C
    row = lax.broadcasted_iota(jnp.int32, (c, c), 0)
    col = lax.broadcasted_iota(jnp.int32, (c, c), 1)
    diff = (row - col).astype(F32)
    pos = lax.broadcasted_iota(jnp.int32, (c, RET_DV), 0).astype(F32)
    for hd in range(RET_HEADS):
        log_gamma = float(np.log1p(-np.exp2(-5.0 - hd)))
        decay_ref[hd] = jnp.where(diff >= 0, jnp.exp(log_gamma * jnp.maximum(diff, 0.0)), 0.0)
        xi_ref[hd] = jnp.exp(log_gamma * (pos + 1.0))
        zeta_ref[hd] = jnp.exp(log_gamma * (c - 1.0 - pos))


def _retention_rows(first_row, n_rows, q_ref, k_ref, v_ref, rg_ref, rn_ref, o_ref,
                    state_ref, decay_ref, xi_ref, zeta_ref):
    c = RET_C
    for ch in range(n_rows // c):
        rows = slice(first_row + ch * c, first_row + (ch + 1) * c)
        for hd in range(RET_HEADS):
            log_gamma = float(np.log1p(-np.exp2(-5.0 - hd)))
            sl = slice(hd * RET_DK, (hd + 1) * RET_DK)
            q = q_ref[rows, sl]
            k = k_ref[rows, sl]
            v = v_ref[rows, sl]
            scores = _dot_nt(q, k) * decay_ref[hd]
            inner = _dot(scores.astype(BF16), v)
            state = state_ref[hd]
            cross = _dot(q, state.astype(BF16)) * xi_ref[hd]
            vz = (v.astype(F32) * zeta_ref[hd]).astype(BF16)
            state_ref[hd] = float(np.exp(log_gamma * c)) * state + _dot_tn(k, vz)
            ret = inner + cross
            ret = ret * lax.rsqrt(jnp.mean(ret * ret, axis=-1, keepdims=True) + EPS)
            gate = jax.nn.silu(rg_ref[rows, sl].astype(F32))
            o_ref[rows, sl] = (gate * (ret * rn_ref[:, sl])).astype(BF16)


def _fox_kernel(qt_ref, qaug_ref, k_ref, vt_ref, o_ref,
                w_ref, s_ref, s2_ref, c_ref, c2_ref, p_ref, a_ref, m_ref, acc_ref):
    tq, tk = FOX_TQ, FOX_TK
    qi = pl.program_id(2)
    for j in range(FOX_GROUP):
        w_ref[j, 0:FOX_DH, :] = qt_ref[0, j * FOX_DH:(j + 1) * FOX_DH, :]
        w_ref[j, FOX_DH:FOX_DH + BF16_ROWS, :] = qaug_ref[0, j * BF16_ROWS:(j + 1) * BF16_ROWS, :]
        w_ref[j, FOX_DH + BF16_ROWS:LANES, :] = jnp.zeros((LANES - FOX_DH - BF16_ROWS, tq), BF16)
    m_ref[...] = jnp.full_like(m_ref, NEG)
    acc_ref[...] = jnp.zeros_like(acc_ref)
    p_ref[...] = jnp.zeros_like(p_ref)
    a_ref[...] = jnp.ones_like(a_ref)

    def rows_of(kv):
        return pl.ds(pl.multiple_of(kv * tk, tk), tk)

    def score_head(j, kv, dst):
        dst_s, dst_c = dst
        rows = rows_of(kv)
        s = _dot(k_ref[rows, j * LANES:(j + 1) * LANES], w_ref[j])
        dst_s[j] = s
        dst_c[j] = jnp.max(s, axis=0, keepdims=True)

    def softmax_head(j, src, diag):
        src_s, src_c = src
        s = src_s[j]
        if diag is not None:
            r = lax.broadcasted_iota(jnp.int32, (tk, tq), 0)
            cidx = lax.broadcasted_iota(jnp.int32, (tk, tq), 1)
            s = jnp.where(r + diag * tk <= cidx, s, NEG)
            cmax = jnp.max(s, axis=0, keepdims=True)
        else:
            cmax = src_c[j]
        m_old = m_ref[j]
        m_new = jnp.maximum(m_old, cmax)
        alpha = jnp.exp2(m_old - m_new)
        p = jnp.exp2(s - m_new)
        m_ref[j] = m_new
        a_ref[j] = alpha
        p_ref[j] = p.astype(BF16)

    def value_head(j, kv):
        rows = rows_of(kv)
        ones = jnp.ones((BF16_ROWS, tk), BF16)
        vt = jnp.concatenate([vt_ref[0, j * FOX_DH:(j + 1) * FOX_DH, rows], ones], axis=0)
        acc_ref[j] = a_ref[j] * acc_ref[j] + _dot(vt, p_ref[j])

    def score_stage(kv, dst):
        for j in range(FOX_GROUP):
            score_head(j, kv, dst)

    def softmax_stage(src, diag):
        for j in range(FOX_GROUP):
            softmax_head(j, src, diag)

    def value_stage(kv):
        for j in range(FOX_GROUP):
            value_head(j, kv)

    def step(i, src, dst, diag=None):
        prev = jnp.maximum(i - 1, 0)
        for j in range(FOX_GROUP):
            score_head(j, i + 1, dst)
            value_head(j, prev)
            softmax_head(j, src, diag)

    buf_a = (s_ref, c_ref)
    buf_b = (s2_ref, c2_ref)

    def body(h, carry):
        step(2 * h, buf_a, buf_b)
        step(2 * h + 1, buf_b, buf_a)
        return carry

    assert tq == 2 * tk
    first_diag = 2 * qi
    score_stage(0, buf_a)

    def body_pair(t, carry):
        body(2 * t, carry)
        body(2 * t + 1, carry)
        return carry

    lax.fori_loop(0, qi // 2, body_pair, 0)

    @pl.when(qi % 2 == 1)
    def _():
        body(qi - 1, 0)

    step(first_diag, buf_a, buf_b, diag=0)

    @pl.when(qi >= 0)
    def _():
        value_stage(first_diag)
        softmax_stage(buf_b, 1)

    @pl.when(qi >= 0)
    def _():
        value_stage(first_diag + 1)
    out_t = jnp.concatenate(
        [acc_ref[j, 0:FOX_DH, :] / acc_ref[j, FOX_DH:FOX_DH + 1, :] for j in range(FOX_GROUP)], axis=0)
    o_ref[...] = out_t.T.astype(BF16)


def _fox(fqt, qaug, fka, fvt, batch, seq):
    t = fka.shape[0]
    tq, tk = FOX_TQ, FOX_TK
    nq = seq // tq
    gw = FOX_GROUP * FOX_DH
    return pl.pallas_call(
        _fox_kernel,
        grid=(batch, FOX_HEADS // FOX_GROUP, nq),
        in_specs=[
            pl.BlockSpec((1, gw, tq), lambda b, g, i: (b, g, i)),
            pl.BlockSpec((1, FOX_GROUP * BF16_ROWS, tq), lambda b, g, i: (b, g, i)),
            pl.BlockSpec((seq, FOX_GROUP * LANES), lambda b, g, i: (b, g)),
            pl.BlockSpec((1, gw, seq), lambda b, g, i: (b, g, 0)),
        ],
        out_specs=pl.BlockSpec((tq, gw), lambda b, g, i: (b * nq + i, g)),
        out_shape=jax.ShapeDtypeStruct((t, FOX_W), BF16),
        scratch_shapes=[pltpu.VMEM((FOX_GROUP, LANES, tq), BF16),
                        pltpu.VMEM((FOX_GROUP, tk, tq), F32),
                        pltpu.VMEM((FOX_GROUP, tk, tq), F32),
                        pltpu.VMEM((FOX_GROUP, 1, tq), F32),
                        pltpu.VMEM((FOX_GROUP, 1, tq), F32),
                        pltpu.VMEM((FOX_GROUP, tk, tq), BF16),
                        pltpu.VMEM((FOX_GROUP, 1, tq), F32),
                        pltpu.VMEM((FOX_GROUP, 1, tq), F32),
                        pltpu.VMEM((FOX_GROUP, FOX_DH + BF16_ROWS, tq), F32)],
        compiler_params=_params("arbitrary", "arbitrary", "arbitrary"),
        name="fox_attn",
    )(fqt, qaug, fka, fvt)


def _wt_kernel(w_ref, o_ref):
    o_ref[...] = w_ref[...].T.astype(BF16)


def _transpose_cast(w_in):
    first = 4 * RET_W // LANES
    return pl.pallas_call(
        _wt_kernel,
        grid=(DEPTH, WT_COLS // LANES),
        in_specs=[pl.BlockSpec((None, D_MODEL, LANES), lambda l, j: (l, 0, first + j))],
        out_specs=pl.BlockSpec((None, LANES, D_MODEL), lambda l, j: (l, j, 0)),
        out_shape=jax.ShapeDtypeStruct((DEPTH, WT_COLS, D_MODEL), BF16),
        compiler_params=_params("arbitrary", "arbitrary"),
        name="w_transpose",
    )(w_in)


def _rope_tables(seq):
    d = RET_DK
    inv = jnp.power(ROPE_BASE, -jnp.arange(0, d, 2, dtype=F32) / d)
    ang = jnp.arange(seq, dtype=F32)[:, None] * inv[None, :]
    cos, sin = jnp.cos(ang), jnp.sin(ang)
    return jnp.concatenate([cos, cos], axis=-1), jnp.concatenate([-sin, sin], axis=-1)


def kernel(x, norm_ffn1, w_ffn1_in, w_ffn1_out, norm_mix, w_in, b_forget, ret_norm,
           w_o_ret, w_o_fox, w_out, norm_ffn2, w_ffn2_in, w_ffn2_out, norm_final):
    batch, seq, d = x.shape
    assert d == D_MODEL and seq % PROJ_TM == 0 and seq % FOX_TQ == 0 and PROJ_SUB % RET_C == 0
    assert all((batch * seq) % tile == 0 for tile in (FFN_TM, MIX_FFN_TM))
    xt = x.reshape(batch * seq, d)

    w1i = w_ffn1_in.astype(BF16)
    w1o = w_ffn1_out.astype(BF16)
    w2i = w_ffn2_in.astype(BF16)
    w2o = w_ffn2_out.astype(BF16)
    fq_off = 4 * RET_W
    w_main = jnp.concatenate([w_in[..., :fq_off], w_in[..., FF_OFF + FOX_HEADS:]], axis=-1).astype(BF16)
    wqvf_t = _transpose_cast(w_in)
    brow = jnp.pad(b_forget, ((0, 0), (0, BF16_ROWS - FOX_HEADS)))[:, :, None]
    wor = w_o_ret.astype(BF16)
    wof = w_o_fox.astype(BF16)
    wo = w_out.astype(BF16)
    cos, sin = _rope_tables(seq)
    g_final = norm_final[None, :]
    g_ffn1 = norm_ffn1[:, None, :]
    g_mix = norm_mix[:, None, :]
    g_ffn2 = norm_ffn2[:, None, :]
    g_ret = ret_norm[:, None, :]

    for l in range(DEPTH):
        xt = _ffn(xt, g_ffn1, w1i, w1o, g_final, l, False)
        ret, gr, gf, fqt, fvt, fka, qaug = _proj(
            xt, g_mix, w_main, wqvf_t, brow, cos, sin, g_ret, l, batch, seq)
        fox = _fox(fqt, qaug, fka, fvt, batch, seq)
        xt = _ffn(xt, g_ffn2, w2i, w2o, g_final, l, l == DEPTH - 1,
                  mixer=(ret, fox, gr, gf, wor, wof, wo))
    return xt.reshape(batch, seq, d)
```
